```python
import math
import jax, jax.numpy as jnp
from jax import lax
import numpy as np

D_MODEL = 1024
BATCH = 8
SEQ = 4096
DEPTH = 1

MIX_WIDTH = D_MODEL
N_MLSTM_HEADS = 4
MLSTM_HEAD_DIM = 128
MLSTM_WIDTH = N_MLSTM_HEADS * MLSTM_HEAD_DIM
N_SB_HEADS = 8
SB_HEAD_DIM = 64
SB_WIDTH = N_SB_HEADS * SB_HEAD_DIM
CONV_WIDTH = 4
MLSTM_CHUNK = 128
SB_BLOCK = 128
PROJ_WIDTH = 4 * MLSTM_WIDTH + 2 * N_MLSTM_HEADS + 3 * SB_WIDTH
PROJ_SPLITS = (2 * MLSTM_WIDTH, 3 * MLSTM_WIDTH, 4 * MLSTM_WIDTH,
               4 * MLSTM_WIDTH + N_MLSTM_HEADS, 4 * MLSTM_WIDTH + 2 * N_MLSTM_HEADS)
MEM_TOKENS = 256
N_XATTN_HEADS = 4
XATTN_HEAD_DIM = D_MODEL // N_XATTN_HEADS
PEER_HEADS = 8
PEER_N_KEYS = 128
PEER_N_EXPERTS = PEER_N_KEYS * PEER_N_KEYS
PEER_TOPK = 16
PEER_QUERY_DIM = 256
PEER_HALF = PEER_QUERY_DIM // 2
PEER_CHUNK = 128
EPS = 1e-6

kernel_name = 'hymba_mlstm_stickbreak_peer_block'


def rms_norm(x, g):
    xf = x.astype(jnp.float32)
    y = xf * lax.rsqrt(jnp.mean(xf * xf, axis=-1, keepdims=True) + EPS) * g.astype(jnp.float32)
    return y.astype(x.dtype)


def head_rms_norm(a, g):
    H, dh = a.shape[1], a.shape[3]
    y = a * lax.rsqrt(jnp.mean(a * a, axis=-1, keepdims=True) + EPS)
    return y * g.astype(jnp.float32).reshape(H, 1, dh)


def split_heads(a, n):
    B_, S, W = a.shape
    return a.reshape(B_, S, n, W // n).transpose(0, 2, 1, 3)


def merge_heads(a):
    B_, H, S, dh = a.shape
    return a.transpose(0, 2, 1, 3).reshape(B_, S, H * dh)


def causal_depthwise_conv(a, w, b):
    C = a.shape[-1]
    out = lax.conv_general_dilated(a, w[:, None, :].astype(a.dtype), window_strides=(1,),
                                   padding=[(CONV_WIDTH - 1, 0)],
                                   dimension_numbers=('NWC', 'WIO', 'NWC'),
                                   feature_group_count=C)
    return out + b.astype(a.dtype)


def mlstm_chunkwise(q, k, v, log_i, log_f):
    B_, H, S, dh = q.shape
    L = MLSTM_CHUNK
    nc = S // L

    def to_chunks(a):
        return jnp.moveaxis(a.reshape(B_, H, nc, L, *a.shape[3:]), 2, 0)

    xs = (to_chunks(q), to_chunks(k), to_chunks(v), to_chunks(log_i), to_chunks(log_f))
    causal = jnp.tril(jnp.ones((L, L), dtype=bool))

    def step(carry, inp):
        C, n, m = carry
        qb, kb, vb, ib, fb = inp
        b = jnp.cumsum(fb, axis=-1)
        g = b + m[..., None]
        Dm = jnp.where(causal, b[..., :, None] - b[..., None, :] + ib[..., None, :], -jnp.inf)
        m_t = jnp.maximum(g, jnp.max(Dm, axis=-1))
        W = jnp.exp(Dm - m_t[..., None])
        inter = jnp.exp(g - m_t)
        qk = jnp.einsum('bhtd,bhsd->bhts', qb, kb) * W
        num = inter[..., None] * jnp.einsum('bhvk,bhtk->bhtv', C, qb) + jnp.einsum('bhts,bhsv->bhtv', qk, vb)
        den = inter * jnp.einsum('bhk,bhtk->bht', n, qb) + jnp.sum(qk, axis=-1)
        h = num / jnp.maximum(jnp.abs(den), jnp.exp(-m_t))[..., None]
        F = b[..., -1]
        a = F[..., None] - b + ib
        m_new = jnp.maximum(F + m, jnp.max(a, axis=-1))
        decay = jnp.exp(F + m - m_new)
        w_s = jnp.exp(a - m_new[..., None])
        C_new = decay[..., None, None] * C + jnp.einsum('bhsv,bhsk->bhvk', vb * w_s[..., None], kb)
        n_new = decay[..., None] * n + jnp.einsum('bhs,bhsk->bhk', w_s, kb)
        return (C_new, n_new, m_new), h

    init = (jnp.zeros((B_, H, dh, dh), jnp.float32), jnp.zeros((B_, H, dh), jnp.float32),
            jnp.zeros((B_, H), jnp.float32))
    _, hs = lax.scan(step, init, xs)
    return jnp.moveaxis(hs, 0, 2).reshape(B_, H, S, dh)


def stick_breaking_attention(q, k, v):
    B_, H, S, dh = q.shape
    L = SB_BLOCK
    scale = dh ** -0.5
    outs = []
    for blk in range(S // L):
        n_prefix = (blk + 1) * L
        q_blk = q[:, :, blk * L:(blk + 1) * L]
        k_pre = k[:, :, :n_prefix]
        v_pre = v[:, :, :n_prefix]
        z = jnp.einsum('bhtd,bhsd->bhts', q_blk, k_pre) * scale
        t_idx = blk * L + jnp.arange(L)[:, None]
        s_idx = jnp.arange(n_prefix)[None, :]
        strict = s_idx < t_idx
        log_1m_beta = jnp.where(strict, jax.nn.log_sigmoid(-z), 0.0)
        log_a = z + lax.cumsum(log_1m_beta, axis=3, reverse=True)
        A = jnp.exp(jnp.where(strict, log_a, -jnp.inf))
        outs.append(jnp.einsum('bhts,bhsd->bhtd', A, v_pre))
    return jnp.concatenate(outs, axis=2)


def parallel_mixer(h, w_in, conv_w, conv_b, igate_b, fgate_b, mlstm_norm_g, sb_norm_g, w_out):
    dtype = h.dtype
    f32 = jnp.float32
    proj = h @ w_in
    qk_m, v_m, o_m, i_pre, f_pre, qkv_s = jnp.split(proj, PROJ_SPLITS, axis=-1)
    qk_m = jax.nn.silu(causal_depthwise_conv(qk_m, conv_w, conv_b))
    q_m, k_m = jnp.split(qk_m, 2, axis=-1)
    q_m = split_heads(q_m, N_MLSTM_HEADS).astype(f32)
    k_m = split_heads(k_m, N_MLSTM_HEADS).astype(f32) * (MLSTM_HEAD_DIM ** -0.5)
    v_m = split_heads(v_m, N_MLSTM_HEADS).astype(f32)
    log_i = jnp.swapaxes((i_pre + igate_b).astype(f32), 1, 2)
    log_f = jax.nn.log_sigmoid(jnp.swapaxes((f_pre + fgate_b).astype(f32), 1, 2))
    h_m = mlstm_chunkwise(q_m, k_m, v_m, log_i, log_f)
    h_m = jax.nn.sigmoid(split_heads(o_m, N_MLSTM_HEADS).astype(f32)) * h_m
    h_m = head_rms_norm(h_m, mlstm_norm_g)
    q_s, k_s, v_s = jnp.split(qkv_s, 3, axis=-1)
    h_s = stick_breaking_attention(split_heads(q_s, N_SB_HEADS).astype(f32),
                                   split_heads(k_s, N_SB_HEADS).astype(f32),
                                   split_heads(v_s, N_SB_HEADS).astype(f32))
    h_s = head_rms_norm(h_s, sb_norm_g)
    mixed = jnp.concatenate([merge_heads(h_m), merge_heads(h_s)], axis=-1).astype(dtype)
    return mixed @ w_out


def memory_cross_attention(h, mem_n, wq, wkv, wo):
    B_, S, _ = h.shape
    M = mem_n.shape[1]
    q = (h @ wq).reshape(B_, S, N_XATTN_HEADS, XATTN_HEAD_DIM).astype(jnp.float32)
    k, v = jnp.split(mem_n @ wkv, 2, axis=-1)
    k = k.reshape(B_, M, N_XATTN_HEADS, XATTN_HEAD_DIM).astype(jnp.float32)
    v = v.reshape(B_, M, N_XATTN_HEADS, XATTN_HEAD_DIM).astype(jnp.float32)
    p = jax.nn.softmax(jnp.einsum('bshd,bmhd->bhsm', q, k) * (XATTN_HEAD_DIM ** -0.5), axis=-1)
    o = jnp.einsum('bhsm,bmhd->bshd', p, v).reshape(B_, S, N_XATTN_HEADS * XATTN_HEAD_DIM)
    return o.astype(h.dtype) @ wo


def peer_ffn(h, wq, subkeys, u, v):
    B_, S, D = h.shape
    T = B_ * S
    xt = h.reshape(T, D)
    q = (xt @ wq).reshape(T, PEER_HEADS, 2, PEER_HALF)
    s = jnp.einsum('thpc,hpnc->thpn', q, subkeys).astype(jnp.float32)
    s_top, i_top = lax.top_k(s, PEER_TOPK)
    cand = s_top[:, :, 0, :, None] + s_top[:, :, 1, None, :]
    cand_idx = i_top[:, :, 0, :, None] * PEER_N_KEYS + i_top[:, :, 1, None, :]
    best, pos = lax.top_k(cand.reshape(T, PEER_HEADS, PEER_TOPK * PEER_TOPK), PEER_TOPK)
    experts = jnp.take_along_axis(cand_idx.reshape(T, PEER_HEADS, PEER_TOPK * PEER_TOPK), pos, axis=-1)
    gates = jax.nn.softmax(best, axis=-1).astype(h.dtype)
    nb = T // PEER_CHUNK
    E = PEER_HEADS * PEER_TOPK

    def block(args):
        xb, eb, gb = args
        ub = u[eb]
        vb = v[eb]
        act = jax.nn.gelu(jnp.einsum('cd,ced->ce', xb, ub), approximate=False) * gb
        return jnp.einsum('ce,ced->cd', act, vb)

    y = lax.map(block, (xt.reshape(nb, PEER_CHUNK, D), experts.reshape(nb, PEER_CHUNK, E),
                        gates.reshape(nb, PEER_CHUNK, E)))
    return y.reshape(B_, S, D)


def setup_inputs(seed: int = 0) -> dict:
    key = jax.random.key(seed)
    ks = jax.random.split(key, 24)
    f32 = jnp.float32

    def nrm(k, shape, scale):
        return jax.random.normal(k, shape, f32) * scale

    def gain(k, shape):
        return 1.0 + 0.02 * jax.random.normal(k, shape, f32)

    return {
        'x': nrm(ks[0], (BATCH, SEQ, D_MODEL), 1.0),
        'mem': nrm(ks[1], (BATCH, MEM_TOKENS, D_MODEL), 1.0),
        'mix_norm_g': gain(ks[2], (DEPTH, D_MODEL)),
        'w_in': nrm(ks[3], (DEPTH, D_MODEL, PROJ_WIDTH), D_MODEL ** -0.5),
        'conv_w': nrm(ks[4], (DEPTH, CONV_WIDTH, 2 * MLSTM_WIDTH), CONV_WIDTH ** -0.5),
        'conv_b': nrm(ks[5], (DEPTH, 2 * MLSTM_WIDTH), 0.02),
        'igate_b': nrm(ks[6], (DEPTH, N_MLSTM_HEADS), 0.1),
        'fgate_b': jnp.linspace(3.0, 6.0, N_MLSTM_HEADS, dtype=f32)[None, :] + nrm(ks[7], (DEPTH, N_MLSTM_HEADS), 0.1),
        'mlstm_norm_g': gain(ks[8], (DEPTH, MLSTM_WIDTH)),
        'sb_norm_g': gain(ks[9], (DEPTH, SB_WIDTH)),
        'w_out': nrm(ks[10], (DEPTH, MIX_WIDTH, D_MODEL), MIX_WIDTH ** -0.5),
        'xattn_norm_g': gain(ks[11], (DEPTH, D_MODEL)),
        'mem_norm_g': gain(ks[12], (DEPTH, D_MODEL)),
        'xattn_wq': nrm(ks[13], (DEPTH, D_MODEL, D_MODEL), D_MODEL ** -0.5),
        'xattn_wkv': nrm(ks[14], (DEPTH, D_MODEL, 2 * D_MODEL), D_MODEL ** -0.5),
        'xattn_wo': nrm(ks[15], (DEPTH, D_MODEL, D_MODEL), D_MODEL ** -0.5),
        'ffn_norm_g': gain(ks[16], (DEPTH, D_MODEL)),
        'peer_wq': nrm(ks[17], (DEPTH, D_MODEL, PEER_HEADS * PEER_QUERY_DIM), D_MODEL ** -0.5),
        'peer_subkeys': nrm(ks[18], (DEPTH, PEER_HEADS, 2, PEER_N_KEYS, PEER_HALF), PEER_HALF ** -0.5),
        'peer_u': nrm(ks[19], (DEPTH, PEER_N_EXPERTS, D_MODEL), D_MODEL ** -0.5),
        'peer_v': nrm(ks[20], (DEPTH, PEER_N_EXPERTS, D_MODEL), (PEER_HEADS * PEER_TOPK) ** -0.5),
        'final_norm_g': gain(ks[21], (D_MODEL,)),
    }


def reference(x, mem, mix_norm_g, w_in, conv_w, conv_b, igate_b, fgate_b, mlstm_norm_g, sb_norm_g,
              w_out, xattn_norm_g, mem_norm_g, xattn_wq, xattn_wkv, xattn_wo, ffn_norm_g, peer_wq,
              peer_subkeys, peer_u, peer_v, final_norm_g):
    for l in range(DEPTH):
        x = x + parallel_mixer(rms_norm(x, mix_norm_g[l]), w_in[l], conv_w[l], conv_b[l], igate_b[l],
                               fgate_b[l], mlstm_norm_g[l], sb_norm_g[l], w_out[l])
        x = x + memory_cross_attention(rms_norm(x, xattn_norm_g[l]), rms_norm(mem, mem_norm_g[l]),
                                       xattn_wq[l], xattn_wkv[l], xattn_wo[l])
        x = x + peer_ffn(rms_norm(x, ffn_norm_g[l]), peer_wq[l], peer_subkeys[l], peer_u[l], peer_v[l])
    return rms_norm(x, final_norm_g)
```

```python
import functools
import math

import jax
import jax.numpy as jnp
from jax import lax
from jax.experimental import pallas as pl
from jax.experimental.pallas import tpu as pltpu
from jax.experimental.pallas import tpu_sc as plsc

F32 = jnp.float32
BF16 = jnp.bfloat16
I32 = jnp.int32

EPS = 1e-6
LANES = 128
N_MLSTM_HEADS = 4
MLSTM_HEAD_DIM = 128
MLSTM_WIDTH = N_MLSTM_HEADS * MLSTM_HEAD_DIM
N_SB_HEADS = 8
SB_HEAD_DIM = 64
SB_WIDTH = N_SB_HEADS * SB_HEAD_DIM
CONV_WIDTH = 4
MLSTM_CHUNK = 128
N_XATTN_HEADS = 4
PEER_HEADS = 8
PEER_N_KEYS = 128
PEER_TOPK = 16
PEER_HALF = 128
PEER_E = PEER_HEADS * PEER_TOPK
VMEM_LIMIT = 48 * 1024 * 1024
NEG_INF = float("-inf")


def _cparams(*sem):
    return pltpu.CompilerParams(dimension_semantics=sem, vmem_limit_bytes=VMEM_LIMIT)


def _rms(x, g):
    return x * lax.rsqrt(jnp.mean(x * x, axis=-1, keepdims=True) + EPS) * g


def _sigmoid(x):
    return 1.0 / (1.0 + jnp.exp(-x))


def _log_sigmoid(x):
    return jnp.minimum(x, 0.0) - jnp.log(1.0 + jnp.exp(-jnp.abs(x)))


def _dot(a, b):
    return jnp.dot(a.astype(BF16), b.astype(BF16), preferred_element_type=F32)


def _dot_nt(a, b):
    return lax.dot_general(a.astype(BF16), b.astype(BF16), (((1,), (1,)), ((), ())),
                           preferred_element_type=F32)


def _dot_tn(a, b):
    return lax.dot_general(a.astype(BF16), b.astype(BF16), (((0,), (0,)), ((), ())),
                           preferred_element_type=F32)


def _norm_proj_kernel(x_ref, g_ref, w_ref, *out_refs, splits, emit_norm):
    xn = _rms(x_ref[...], g_ref[...])
    y = jnp.dot(xn.astype(BF16), w_ref[...], preferred_element_type=F32)
    outs = out_refs
    if emit_norm:
        outs[0][...] = xn
        outs = outs[1:]
    off = 0
    for o_ref, width in zip(outs, splits):
        o_ref[...] = y[:, off:off + width]
        off += width


def _norm_proj(x, g, w, splits, emit_norm=False, tm=256):
    T, D = x.shape
    N = w.shape[1]
    assert sum(splits) == N and T % tm == 0
    out_shape = [jax.ShapeDtypeStruct((T, wd), F32) for wd in splits]
    out_specs = [pl.BlockSpec((tm, wd), lambda i: (i, 0)) for wd in splits]
    if emit_norm:
        out_shape = [jax.ShapeDtypeStruct((T, D), F32)] + out_shape
        out_specs = [pl.BlockSpec((tm, D), lambda i: (i, 0))] + out_specs
    return pl.pallas_call(
        functools.partial(_norm_proj_kernel, splits=tuple(splits), emit_norm=emit_norm),
        grid=(T // tm,),
        in_specs=[pl.BlockSpec((tm, D), lambda i: (i, 0)),
                  pl.BlockSpec((1, D), lambda i: (0, 0)),
                  pl.BlockSpec((D, N), lambda i: (0, 0))],
        out_specs=out_specs,
        out_shape=out_shape,
        compiler_params=_cparams("parallel"),
    )(x, g.reshape(1, D), w)


def _mlstm_kernel(qk_ref, v_ref, o_ref, gt_ref, cw_ref, cb_ref, gb_ref, ng_ref, out_ref,
                  ext_ref, ct_ref, n_ref, m_ref):
    L = MLSTM_CHUNK
    dh = MLSTM_HEAD_DIM
    c = pl.program_id(1)

    @pl.when(c == 0)
    def _():
        ext_ref[0:8, :] = jnp.zeros((8, ext_ref.shape[1]), F32)
        ct_ref[...] = jnp.zeros(ct_ref.shape, F32)
        n_ref[...] = jnp.zeros(n_ref.shape, F32)
        m_ref[...] = jnp.zeros(m_ref.shape, F32)

    a = qk_ref[...]
    ext_ref[8:8 + L, :] = a
    conv = cb_ref[...]
    for j in range(CONV_WIDTH):
        conv = conv + cw_ref[j:j + 1, :] * ext_ref[pl.ds(8 - (CONV_WIDTH - 1) + j, L), :]
    ext_ref[0:8, :] = a[L - 8:L, :]
    qk = conv * _sigmoid(conv)

    G = gt_ref[...] + gb_ref[...]
    lane = lax.broadcasted_iota(I32, G.shape, 1)
    LG = jnp.where(lane < N_MLSTM_HEADS, G, _log_sigmoid(G))
    rr = lax.broadcasted_iota(I32, (L, L), 0)
    cc = lax.broadcasted_iota(I32, (L, L), 1)
    causal = cc <= rr
    tril = causal.astype(F32)
    bcum = jnp.dot(tril, LG, preferred_element_type=F32, precision=lax.Precision.HIGHEST)
    LGT = LG.T
    bcumT = bcum.T

    for h in range(N_MLSTM_HEADS):
        q = qk[:, h * dh:(h + 1) * dh]
        k = qk[:, MLSTM_WIDTH + h * dh:MLSTM_WIDTH + (h + 1) * dh] * (dh ** -0.5)
        v = v_ref[:, h * dh:(h + 1) * dh]
        i_col = LG[:, h:h + 1]
        i_row = LGT[h:h + 1, :]
        b_col = bcum[:, N_MLSTM_HEADS + h:N_MLSTM_HEADS + h + 1]
        b_row = bcumT[N_MLSTM_HEADS + h:N_MLSTM_HEADS + h + 1, :]
        m_prev = m_ref[h:h + 1, 0:1]
        ct = ct_ref[h]
        n_row = n_ref[h:h + 1, :]

        g_col = b_col + m_prev
        Dm = jnp.where(causal, b_col - b_row + i_row, NEG_INF)
        m_t = jnp.maximum(g_col, jnp.max(Dm, axis=1, keepdims=True))
        W = jnp.exp(Dm - m_t)
        inter = jnp.exp(g_col - m_t)
        s = _dot_nt(q, k) * W
        num = inter * _dot(q, ct) + _dot(s, v)
        den = inter * jnp.sum(q * n_row, axis=1, keepdims=True) + jnp.sum(s, axis=1, keepdims=True)
        hh = num / jnp.maximum(jnp.abs(den), jnp.exp(-m_t))

        F = b_col[L - 1:L, :]
        a_col = F - b_col + i_col
        m_new = jnp.maximum(F + m_prev, jnp.max(a_col, axis=0, keepdims=True))
        decay = jnp.exp(F + m_prev - m_new)
        w_col = jnp.exp(a_col - m_new)
        ct_ref[h] = decay * ct + _dot_tn(k, v * w_col)
        n_ref[h:h + 1, :] = decay * n_row + jnp.sum(k * w_col, axis=0, keepdims=True)
        m_ref[h:h + 1, :] = jnp.broadcast_to(m_new, (1, m_ref.shape[1]))

        hh = _sigmoid(o_ref[:, h * dh:(h + 1) * dh]) * hh
        hh = hh * lax.rsqrt(jnp.mean(hh * hh, axis=1, keepdims=True) + EPS)
        out_ref[:, h * dh:(h + 1) * dh] = hh * ng_ref[:, h * dh:(h + 1) * dh]


def _mlstm(qk_pre, v_m, o_m, gates, conv_w, conv_b, gate_b, norm_g, B, S):
    L = MLSTM_CHUNK
    nc = S // L
    W2 = 2 * MLSTM_WIDTH
    row = lambda b, c: (b * nc + c, 0)
    const = lambda b, c: (0, 0)
    return pl.pallas_call(
        _mlstm_kernel,
        grid=(B, nc),
        in_specs=[pl.BlockSpec((L, W2), row),
                  pl.BlockSpec((L, MLSTM_WIDTH), row),
                  pl.BlockSpec((L, MLSTM_WIDTH), row),
                  pl.BlockSpec((L, LANES), row),
                  pl.BlockSpec((CONV_WIDTH, W2), const),
                  pl.BlockSpec((1, W2), const),
                  pl.BlockSpec((1, LANES), const),
                  pl.BlockSpec((1, MLSTM_WIDTH), const)],
        out_specs=pl.BlockSpec((L, MLSTM_WIDTH), row),
        out_shape=jax.ShapeDtypeStruct((B * S, MLSTM_WIDTH), F32),
        scratch_shapes=[pltpu.VMEM((8 + L, W2), F32),
                        pltpu.VMEM((N_MLSTM_HEADS, MLSTM_HEAD_DIM, MLSTM_HEAD_DIM), F32),
                        pltpu.VMEM((8, MLSTM_HEAD_DIM), F32),
                        pltpu.VMEM((8, LANES), F32)],
        compiler_params=_cparams("parallel", "arbitrary"),
    )(qk_pre, v_m, o_m, gates, conv_w, conv_b.reshape(1, W2), gate_b.reshape(1, LANES),
      norm_g.reshape(1, MLSTM_WIDTH))


def _sb_kernel(q_ref, k_ref, v_ref, g_ref, out_ref, *, tq):
    tk = tq
    i = pl.program_id(2)
    d = SB_HEAD_DIM
    lane = lax.broadcasted_iota(I32, (1, LANES), 1)
    first = lane < d
    q = q_ref[...] * (d ** -0.5)
    qh = (jnp.where(first, q, 0.0).astype(BF16), jnp.where(first, 0.0, q).astype(BF16))
    rr = lax.broadcasted_iota(I32, (tk, tk), 0)
    cc = lax.broadcasted_iota(I32, (tk, tk), 1)
    tri = (rr >= cc).astype(BF16)
    strict = cc < rr

    def tile(j, carry, diag):
        acc, r0, r1 = carry
        start = pl.multiple_of(j * tk, tk)
        kb = k_ref[pl.ds(start, tk), :].astype(BF16)
        vb = v_ref[pl.ds(start, tk), :]
        vh = (jnp.where(first, vb, 0.0).astype(BF16), jnp.where(first, 0.0, vb).astype(BF16))
        new_r = []
        for h, r in ((0, r0), (1, r1)):
            z = lax.dot_general(qh[h], kb, (((1,), (1,)), ((), ())), preferred_element_type=F32)
            lsm = -(jnp.maximum(z, 0.0) + jnp.log(1.0 + jnp.exp(-jnp.abs(z))))
            if diag:
                lsm = jnp.where(strict, lsm, 0.0)
            hi = lsm.astype(BF16)
            lo = (lsm - hi.astype(F32)).astype(BF16)
            cs = (jnp.dot(hi, tri, preferred_element_type=F32)
                  + jnp.dot(lo, tri, preferred_element_type=F32))
            log_a = z + cs + r
            if diag:
                log_a = jnp.where(strict, log_a, NEG_INF)
            p = jnp.exp(log_a)
            acc = acc + jnp.dot(p.astype(BF16), vh[h], preferred_element_type=F32)
            new_r.append(r + cs[:, 0:1])
        return acc, new_r[0], new_r[1]

    zero_r = jnp.zeros((tq, 1), F32)
    carry = tile(i, (jnp.zeros((tq, LANES), F32), zero_r, zero_r), True)
    carry = lax.fori_loop(0, i, lambda jj, cr: tile(i - 1 - jj, cr, False), carry)
    acc = carry[0]
    sq = acc * acc
    ms0 = jnp.sum(jnp.where(first, sq, 0.0), axis=1, keepdims=True) * (1.0 / d)
    ms1 = jnp.sum(jnp.where(first, 0.0, sq), axis=1, keepdims=True) * (1.0 / d)
    ms = jnp.where(first, ms0, ms1)
    out_ref[...] = acc * lax.rsqrt(ms + EPS) * g_ref[...]


def _sb_attention(qkv_s, norm_g, B, S, tq=128):
    nq = S // tq
    npair = SB_WIDTH // LANES
    return pl.pallas_call(
        functools.partial(_sb_kernel, tq=tq),
        grid=(B, npair, nq),
        in_specs=[pl.BlockSpec((tq, LANES), lambda b, p, i: (b * nq + i, p)),
                  pl.BlockSpec((S, LANES), lambda b, p, i: (b, npair + p)),
                  pl.BlockSpec((S, LANES), lambda b, p, i: (b, 2 * npair + p)),
                  pl.BlockSpec((1, LANES), lambda b, p, i: (0, p))],
        out_specs=pl.BlockSpec((tq, LANES), lambda b, p, i: (b * nq + i, p)),
        out_shape=jax.ShapeDtypeStruct((B * S, SB_WIDTH), F32),
        compiler_params=_cparams("parallel", "parallel", "arbitrary"),
    )(qkv_s, qkv_s, qkv_s, norm_g.reshape(1, SB_WIDTH))


def _xattn_kernel(x_ref, hm_ref, hs_ref, wo1_ref, wo2_ref, g_ref, wq_ref, kv_ref, wo_ref, out_ref):
    D = x_ref.shape[1]
    dh = D // N_XATTN_HEADS
    x1 = (x_ref[...]
          + jnp.dot(hm_ref[...].astype(BF16), wo1_ref[...], preferred_element_type=F32)
          + jnp.dot(hs_ref[...].astype(BF16), wo2_ref[...], preferred_element_type=F32))
    xn = _rms(x1, g_ref[...])
    q = jnp.dot(xn.astype(BF16), wq_ref[...], preferred_element_type=F32)
    heads = []
    for h in range(N_XATTN_HEADS):
        qh = q[:, h * dh:(h + 1) * dh]
        kh = kv_ref[:, h * dh:(h + 1) * dh]
        vh = kv_ref[:, D + h * dh:D + (h + 1) * dh]
        s = _dot_nt(qh, kh) * (dh ** -0.5)
        s = s - jnp.max(s, axis=1, keepdims=True)
        p = jnp.exp(s)
        p = p / jnp.sum(p, axis=1, keepdims=True)
        heads.append(_dot(p, vh))
    o = jnp.concatenate(heads, axis=1)
    out_ref[...] = x1 + jnp.dot(o.astype(BF16), wo_ref[...], preferred_element_type=F32)


def _mixer_out_xattn(x, hm, hs, wo1, wo2, g, wq, kv, wo, B, S, M, tm=256):
    T, D = x.shape
    nt = S // tm
    row = lambda i: (i, 0)
    const = lambda i: (0, 0)
    return pl.pallas_call(
        _xattn_kernel,
        grid=(T // tm,),
        in_specs=[pl.BlockSpec((tm, D), row),
                  pl.BlockSpec((tm, MLSTM_WIDTH), row),
                  pl.BlockSpec((tm, SB_WIDTH), row),
                  pl.BlockSpec((MLSTM_WIDTH, D), const),
                  pl.BlockSpec((SB_WIDTH, D), const),
                  pl.BlockSpec((1, D), const),
                  pl.BlockSpec((D, D), const),
                  pl.BlockSpec((M, 2 * D), lambda i: (i // nt, 0)),
                  pl.BlockSpec((D, D), const)],
        out_specs=pl.BlockSpec((tm, D), row),
        out_shape=jax.ShapeDtypeStruct((T, D), F32),
        compiler_params=_cparams("parallel"),
    )(x, hm, hs, wo1, wo2, g.reshape(1, D), wq, kv, wo)


def _top_rows(s, ids, payload, k):
    big = jnp.int32(2 ** 30)
    vals, pays = [], []
    for _ in range(k):
        m = jnp.max(s, axis=0, keepdims=True)
        sel = jnp.min(jnp.where(s == m, ids, big), axis=0, keepdims=True)
        hit = ids == sel
        vals.append(m)
        pays.append(jnp.sum(jnp.where(hit, payload, 0), axis=0, keepdims=True))
        s = jnp.where(hit, NEG_INF, s)
    return jnp.concatenate(vals, axis=0), jnp.concatenate(pays, axis=0)


def _peer_route_kernel(q_ref, sk_ref, exp_ref, gate_ref):
    tm = q_ref.shape[0]
    K = PEER_TOPK
    key_ids = lax.broadcasted_iota(I32, (PEER_N_KEYS, tm), 0)
    exp_rows, gate_rows = [], []
    for h in range(PEER_HEADS):
        tops = []
        for p in range(2):
            hp = 2 * h + p
            qc = q_ref[:, hp * PEER_HALF:(hp + 1) * PEER_HALF]
            s = _dot_nt(sk_ref[hp], qc)
            tops.append(_top_rows(s, key_ids, key_ids, K))
        (v0, i0), (v1, i1) = tops
        cand, cpos, cexp = [], [], []
        for a in range(K):
            nb = K // (a + 1)
            cand.append(v0[a:a + 1, :] + v1[0:nb, :])
            cpos.append(a * K + lax.broadcasted_iota(I32, (nb, tm), 0))
            cexp.append(i0[a:a + 1, :] * PEER_N_KEYS + i1[0:nb, :])
        best, experts = _top_rows(jnp.concatenate(cand, axis=0), jnp.concatenate(cpos, axis=0),
                                  jnp.concatenate(cexp, axis=0), K)
        e = jnp.exp(best - best[0:1, :])
        gate_rows.append(e / jnp.sum(e, axis=0, keepdims=True))
        exp_rows.append(experts)
    exp_ref[0] = jnp.concatenate(exp_rows, axis=0)
    gate_ref[0] = jnp.concatenate(gate_rows, axis=0)


def _peer_route(q, subkeys, tm=128):
    T = q.shape[0]
    nb = T // tm
    return pl.pallas_call(
        _peer_route_kernel,
        grid=(nb,),
        in_specs=[pl.BlockSpec((tm, q.shape[1]), lambda i: (i, 0)),
                  pl.BlockSpec(subkeys.shape, lambda i: (0, 0, 0))],
        out_specs=[pl.BlockSpec((1, PEER_E, tm), lambda i: (i, 0, 0)),
                   pl.BlockSpec((1, PEER_E, tm), lambda i: (i, 0, 0))],
        out_shape=[jax.ShapeDtypeStruct((nb, PEER_E, tm), I32),
                   jax.ShapeDtypeStruct((nb, PEER_E, tm), F32)],
        compiler_params=_cparams("parallel"),
    )(q, subkeys)


SC_GATHER_WINDOW = 128


def _gather_rows(table, idx):
    N = idx.shape[0]
    W = table.shape[1]
    win = SC_GATHER_WINDOW
    assert N % win == 0
    mesh = plsc.VectorSubcoreMesh(core_axis_name="core", subcore_axis_name="subcore")

    @functools.partial(pl.kernel, out_type=jax.ShapeDtypeStruct((N, W), table.dtype), mesh=mesh)
    def gather_kernel(table_hbm, idx_hbm, out_hbm):
        def body(idx_vmem, out_vmem):
            pltpu.sync_copy(table_hbm.at[idx_vmem.at[0]], out_vmem)

        pltpu.emit_pipeline(
            body,
            grid=(N // win,),
            in_specs=[pl.BlockSpec((1, win), lambda i: (0, i))],
            out_specs=[pl.BlockSpec((win, W), lambda i: (i, 0))],
            core_axis_name=("core", "subcore"),
            dimension_semantics=(pltpu.PARALLEL,),
        )(idx_hbm, out_hbm)

    return gather_kernel(table, idx.reshape(1, N))


PEER_TOKENS_PER_STEP = 8


def _unpack_pair(words):
    lo = lax.bitcast_convert_type(lax.shift_left(words, jnp.int32(16)), F32)
    hi = lax.bitcast_convert_type(jnp.bitwise_and(words, jnp.int32(-65536)), F32)
    return lo, hi


def _peer_ffn_kernel(x_ref, xn_ref, gate_ref, ua_ref, ub_ref, va_ref, vb_ref, fg_ref, out_ref):
    tb = x_ref.shape[0]
    E = PEER_E
    Q = xn_ref.shape[1] // 4
    tm = gate_ref.shape[2]
    lane0 = (pl.program_id(0) * tb) % tm
    G = gate_ref[0]
    lane = lax.broadcasted_iota(I32, G.shape, 1)
    rows = []
    for t in range(tb):
        xr = xn_ref[t:t + 1, :]
        gcol = jnp.sum(jnp.where(lane == lane0 + t, G, 0.0), axis=1, keepdims=True)
        u0, u1 = _unpack_pair(ua_ref[t * E:(t + 1) * E, :])
        u2, u3 = _unpack_pair(ub_ref[t * E:(t + 1) * E, :])
        prod = (u0 * xr[:, 0:Q] + u1 * xr[:, Q:2 * Q]) + (u2 * xr[:, 2 * Q:3 * Q] + u3 * xr[:, 3 * Q:])
        hcol = jnp.sum(prod, axis=1, keepdims=True)
        act = 0.5 * hcol * (1.0 + lax.erf(hcol * (2.0 ** -0.5))) * gcol
        v0, v1 = _unpack_pair(va_ref[t * E:(t + 1) * E, :])
        v2, v3 = _unpack_pair(vb_ref[t * E:(t + 1) * E, :])
        rows.append(jnp.concatenate([jnp.sum(act * vq, axis=0, keepdims=True) for vq in (v0, v1, v2, v3)],
                                    axis=1))
    y = jnp.concatenate(rows, axis=0)
    out_ref[...] = _rms(x_ref[...] + y, fg_ref[...])


def _peer_ffn(x2, xn, gates_t, ua, ub, va, vb, final_g, row0):
    tb = PEER_TOKENS_PER_STEP
    D = x2.shape[1]
    tm = gates_t.shape[2]
    Tc = ua.shape[0] // PEER_E
    assert row0 % tm == 0 and tm % tb == 0
    blk0 = row0 // tb
    rows_spec = pl.BlockSpec((tb * PEER_E, D // 4), lambda i: (i, 0))
    return pl.pallas_call(
        _peer_ffn_kernel,
        grid=(Tc // tb,),
        in_specs=[pl.BlockSpec((tb, D), lambda i: (blk0 + i, 0)),
                  pl.BlockSpec((tb, D), lambda i: (blk0 + i, 0)),
                  pl.BlockSpec((1, PEER_E, tm), lambda i: ((row0 + i * tb) // tm, 0, 0)),
                  rows_spec, rows_spec, rows_spec, rows_spec,
                  pl.BlockSpec((1, D), lambda i: (0, 0))],
        out_specs=pl.BlockSpec((tb, D), lambda i: (i, 0)),
        out_shape=jax.ShapeDtypeStruct((Tc, D), F32),
        compiler_params=_cparams("parallel"),
    )(x2, xn, gates_t, ua, ub, va, vb, final_g.reshape(1, D))


def _pack_bf16_pairs(w):
    Q = w.shape[1] // 4
    bits = lax.bitcast_convert_type(w.astype(BF16), jnp.uint16).astype(jnp.uint32)
    tables = []
    for s in range(2):
        words = bits[:, 2 * s * Q:(2 * s + 1) * Q] | (bits[:, (2 * s + 1) * Q:(2 * s + 2) * Q] << 16)
        tables.append(lax.bitcast_convert_type(words, I32))
    return tables


PEER_TOKEN_CHUNK = 4096


def kernel(x, mem, mix_norm_g, w_in, conv_w, conv_b, igate_b, fgate_b, mlstm_norm_g, sb_norm_g, w_out, xattn_norm_g, mem_norm_g, xattn_wq, xattn_wkv, xattn_wo, ffn_norm_g, peer_wq, peer_subkeys, peer_u, peer_v, final_norm_g):
    B, S, D = x.shape
    M = mem.shape[1]
    T = B * S
    depth = w_in.shape[0]
    assert depth == 1, "the output norm is fused into the last PEER kernel; only one layer is supported"
    MW = MLSTM_WIDTH
    xt = x.reshape(T, D)
    for l in range(depth):
        w = w_in[l]
        gate_cols = jnp.pad(w[:, 4 * MW:4 * MW + 2 * N_MLSTM_HEADS], ((0, 0), (0, LANES - 2 * N_MLSTM_HEADS)))
        w_all = jnp.concatenate([w[:, :4 * MW], gate_cols, w[:, 4 * MW + 2 * N_MLSTM_HEADS:]], axis=1).astype(BF16)
        qk_pre, v_m, o_m, gates, qkv_s = _norm_proj(
            xt, mix_norm_g[l], w_all, (2 * MW, MW, MW, LANES, 3 * SB_WIDTH))
        gate_b = jnp.pad(jnp.concatenate([igate_b[l], fgate_b[l]]), (0, LANES - 2 * N_MLSTM_HEADS))
        h_m = _mlstm(qk_pre, v_m, o_m, gates, conv_w[l], conv_b[l], gate_b, mlstm_norm_g[l], B, S)
        h_s = _sb_attention(qkv_s, sb_norm_g[l], B, S)

        (kv,) = _norm_proj(mem.reshape(B * M, D), mem_norm_g[l], xattn_wkv[l].astype(BF16), (2 * D,))
        wo = w_out[l].astype(BF16)
        x2 = _mixer_out_xattn(xt, h_m, h_s, wo[:MW], wo[MW:], xattn_norm_g[l], xattn_wq[l].astype(BF16),
                              kv, xattn_wo[l].astype(BF16), B, S, M)

        xn3, q_peer = _norm_proj(x2, ffn_norm_g[l], peer_wq[l].astype(BF16),
                                 (PEER_HEADS * 2 * PEER_HALF,), emit_norm=True)
        subkeys = peer_subkeys[l].reshape(PEER_HEADS * 2, PEER_N_KEYS, PEER_HALF).astype(BF16)
        experts_t, gates_t = _peer_route(q_peer, subkeys)
        E = PEER_E
        experts = experts_t.transpose(0, 2, 1).reshape(T * E)

        ua_t, ub_t = _pack_bf16_pairs(peer_u[l])
        va_t, vb_t = _pack_bf16_pairs(peer_v[l])
        Tc = min(PEER_TOKEN_CHUNK, T)
        outs = []
        for c in range(T // Tc):
            idx_c = lax.slice(experts, (c * Tc * E,), ((c + 1) * Tc * E,))
            rows = [_gather_rows(tbl, idx_c) for tbl in (ua_t, ub_t, va_t, vb_t)]
            outs.append(_peer_ffn(x2, xn3, gates_t, *rows, final_norm_g, c * Tc))
        xt = jnp.concatenate(outs, axis=0) if len(outs) > 1 else outs[0]
    return xt.reshape(B, S, D)
```

```python
import functools
import math

import jax
import jax.numpy as jnp
from jax import lax
from jax.experimental import pallas as pl
from jax.experimental.pallas import tpu as pltpu
from jax.experimental.pallas import tpu_sc as plsc

F32 = jnp.float32
BF16 = jnp.bfloat16
I32 = jnp.int32

EPS = 1e-6
LANES = 128
N_MLSTM_HEADS = 4
MLSTM_HEAD_DIM = 128
MLSTM_WIDTH = N_MLSTM_HEADS * MLSTM_HEAD_DIM
N_SB_HEADS = 8
SB_HEAD_DIM = 64
SB_WIDTH = N_SB_HEADS * SB_HEAD_DIM
CONV_WIDTH = 4
MLSTM_CHUNK = 128
N_XATTN_HEADS = 4
PEER_HEADS = 8
PEER_N_KEYS = 128
PEER_TOPK = 16
PEER_HALF = 128
PEER_E = PEER_HEADS * PEER_TOPK
VMEM_LIMIT = 48 * 1024 * 1024
NEG_INF = float("-inf")


def _cparams(*sem):
    return pltpu.CompilerParams(dimension_semantics=sem, vmem_limit_bytes=VMEM_LIMIT)


def _rms(x, g):
    return x * lax.rsqrt(jnp.mean(x * x, axis=-1, keepdims=True) + EPS) * g


def _sigmoid(x):
    return 1.0 / (1.0 + jnp.exp(-x))


def _log_sigmoid(x):
    return jnp.minimum(x, 0.0) - jnp.log(1.0 + jnp.exp(-jnp.abs(x)))


def _dot(a, b):
    return jnp.dot(a.astype(BF16), b.astype(BF16), preferred_element_type=F32)


def _dot_nt(a, b):
    return lax.dot_general(a.astype(BF16), b.astype(BF16), (((1,), (1,)), ((), ())),
                           preferred_element_type=F32)


def _dot_tn(a, b):
    return lax.dot_general(a.astype(BF16), b.astype(BF16), (((0,), (0,)), ((), ())),
                           preferred_element_type=F32)


def _norm_proj_kernel(x_ref, g_ref, w_ref, *out_refs, splits, emit_norm):
    xn = _rms(x_ref[...], g_ref[...])
    y = jnp.dot(xn.astype(BF16), w_ref[...], preferred_element_type=F32)
    outs = out_refs
    if emit_norm:
        outs[0][...] = xn
        outs = outs[1:]
    off = 0
    for o_ref, width in zip(outs, splits):
        o_ref[...] = y[:, off:off + width]
        off += width


def _norm_proj(x, g, w, splits, emit_norm=False, tm=256):
    T, D = x.shape
    N = w.shape[1]
    assert sum(splits) == N and T % tm == 0
    out_shape = [jax.ShapeDtypeStruct((T, wd), F32) for wd in splits]
    out_specs = [pl.BlockSpec((tm, wd), lambda i: (i, 0)) for wd in splits]
    if emit_norm:
        out_shape = [jax.ShapeDtypeStruct((T, D), F32)] + out_shape
        out_specs = [pl.BlockSpec((tm, D), lambda i: (i, 0))] + out_specs
    return pl.pallas_call(
        functools.partial(_norm_proj_kernel, splits=tuple(splits), emit_norm=emit_norm),
        grid=(T // tm,),
        in_specs=[pl.BlockSpec((tm, D), lambda i: (i, 0)),
                  pl.BlockSpec((1, D), lambda i: (0, 0)),
                  pl.BlockSpec((D, N), lambda i: (0, 0))],
        out_specs=out_specs,
        out_shape=out_shape,
        compiler_params=_cparams("parallel"),
    )(x, g.reshape(1, D), w)


def _mlstm_kernel(qk_ref, v_ref, o_ref, gt_ref, cw_ref, cb_ref, gb_ref, ng_ref, out_ref,
                  ext_ref, ct_ref, n_ref, m_ref):
    L = MLSTM_CHUNK
    dh = MLSTM_HEAD_DIM
    c = pl.program_id(1)

    @pl.when(c == 0)
    def _():
        ext_ref[0:8, :] = jnp.zeros((8, ext_ref.shape[1]), F32)
        ct_ref[...] = jnp.zeros(ct_ref.shape, F32)
        n_ref[...] = jnp.zeros(n_ref.shape, F32)
        m_ref[...] = jnp.zeros(m_ref.shape, F32)

    a = qk_ref[...]
    ext_ref[8:8 + L, :] = a
    conv = cb_ref[...]
    for j in range(CONV_WIDTH):
        conv = conv + cw_ref[j:j + 1, :] * ext_ref[pl.ds(8 - (CONV_WIDTH - 1) + j, L), :]
    ext_ref[0:8, :] = a[L - 8:L, :]
    qk = conv * _sigmoid(conv)

    G = gt_ref[...] + gb_ref[...]
    lane = lax.broadcasted_iota(I32, G.shape, 1)
    LG = jnp.where(lane < N_MLSTM_HEADS, G, _log_sigmoid(G))
    rr = lax.broadcasted_iota(I32, (L, L), 0)
    cc = lax.broadcasted_iota(I32, (L, L), 1)
    causal = cc <= rr
    tril = causal.astype(F32)
    bcum = jnp.dot(tril, LG, preferred_element_type=F32, precision=lax.Precision.HIGHEST)
    LGT = LG.T
    bcumT = bcum.T

    for h in range(N_MLSTM_HEADS):
        q = qk[:, h * dh:(h + 1) * dh]
        k = qk[:, MLSTM_WIDTH + h * dh:MLSTM_WIDTH + (h + 1) * dh] * (dh ** -0.5)
        v = v_ref[:, h * dh:(h + 1) * dh]
        i_col = LG[:, h:h + 1]
        i_row = LGT[h:h + 1, :]
        b_col = bcum[:, N_MLSTM_HEADS + h:N_MLSTM_HEADS + h + 1]
        b_row = bcumT[N_MLSTM_HEADS + h:N_MLSTM_HEADS + h + 1, :]
        m_prev = m_ref[h:h + 1, 0:1]
        ct = ct_ref[h]
        n_row = n_ref[h:h + 1, :]

        g_col = b_col + m_prev
        Dm = jnp.where(causal, b_col - b_row + i_row, NEG_INF)
        m_t = jnp.maximum(g_col, jnp.max(Dm, axis=1, keepdims=True))
        W = jnp.exp(Dm - m_t)
        inter = jnp.exp(g_col - m_t)
        s = _dot_nt(q, k) * W
        num = inter * _dot(q, ct) + _dot(s, v)
        den = inter * jnp.sum(q * n_row, axis=1, keepdims=True) + jnp.sum(s, axis=1, keepdims=True)
        hh = num / jnp.maximum(jnp.abs(den), jnp.exp(-m_t))

        F = b_col[L - 1:L, :]
        a_col = F - b_col + i_col
        m_new = jnp.maximum(F + m_prev, jnp.max(a_col, axis=0, keepdims=True))
        decay = jnp.exp(F + m_prev - m_new)
        w_col = jnp.exp(a_col - m_new)
        ct_ref[h] = decay * ct + _dot_tn(k, v * w_col)
        n_ref[h:h + 1, :] = decay * n_row + jnp.sum(k * w_col, axis=0, keepdims=True)
        m_ref[h:h + 1, :] = jnp.broadcast_to(m_new, (1, m_ref.shape[1]))

        hh = _sigmoid(o_ref[:, h * dh:(h + 1) * dh]) * hh
        hh = hh * lax.rsqrt(jnp.mean(hh * hh, axis=1, keepdims=True) + EPS)
        out_ref[:, h * dh:(h + 1) * dh] = hh * ng_ref[:, h * dh:(h + 1) * dh]


def _mlstm(qk_pre, v_m, o_m, gates, conv_w, conv_b, gate_b, norm_g, B, S):
    L = MLSTM_CHUNK
    nc = S // L
    W2 = 2 * MLSTM_WIDTH
    row = lambda b, c: (b * nc + c, 0)
    const = lambda b, c: (0, 0)
    return pl.pallas_call(
        _mlstm_kernel,
        grid=(B, nc),
        in_specs=[pl.BlockSpec((L, W2), row),
                  pl.BlockSpec((L, MLSTM_WIDTH), row),
                  pl.BlockSpec((L, MLSTM_WIDTH), row),
                  pl.BlockSpec((L, LANES), row),
                  pl.BlockSpec((CONV_WIDTH, W2), const),
                  pl.BlockSpec((1, W2), const),
                  pl.BlockSpec((1, LANES), const),
                  pl.BlockSpec((1, MLSTM_WIDTH), const)],
        out_specs=pl.BlockSpec((L, MLSTM_WIDTH), row),
        out_shape=jax.ShapeDtypeStruct((B * S, MLSTM_WIDTH), F32),
        scratch_shapes=[pltpu.VMEM((8 + L, W2), F32),
                        pltpu.VMEM((N_MLSTM_HEADS, MLSTM_HEAD_DIM, MLSTM_HEAD_DIM), F32),
                        pltpu.VMEM((8, MLSTM_HEAD_DIM), F32),
                        pltpu.VMEM((8, LANES), F32)],
        compiler_params=_cparams("parallel", "arbitrary"),
    )(qk_pre, v_m, o_m, gates, conv_w, conv_b.reshape(1, W2), gate_b.reshape(1, LANES),
      norm_g.reshape(1, MLSTM_WIDTH))


def _sb_kernel(q_ref, k_ref, v_ref, g_ref, out_ref, *, tq):
    tk = tq
    i = pl.program_id(2)
    d = SB_HEAD_DIM
    lane = lax.broadcasted_iota(I32, (1, LANES), 1)
    first = lane < d
    q = q_ref[...] * (d ** -0.5)
    qh = (jnp.where(first, q, 0.0).astype(BF16), jnp.where(first, 0.0, q).astype(BF16))
    rr = lax.broadcasted_iota(I32, (tk, tk), 0)
    cc = lax.broadcasted_iota(I32, (tk, tk), 1)
    tri = (rr >= cc).astype(BF16)
    strict = cc < rr

    def tile(j, carry, diag):
        acc, r0, r1 = carry
        start = pl.multiple_of(j * tk, tk)
        kb = k_ref[pl.ds(start, tk), :].astype(BF16)
        vb = v_ref[pl.ds(start, tk), :]
        vh = (jnp.where(first, vb, 0.0).astype(BF16), jnp.where(first, 0.0, vb).astype(BF16))
        new_r = []
        for h, r in ((0, r0), (1, r1)):
            z = lax.dot_general(qh[h], kb, (((1,), (1,)), ((), ())), preferred_element_type=F32)
            lsm = -(jnp.maximum(z, 0.0) + jnp.log(1.0 + jnp.exp(-jnp.abs(z))))
            if diag:
                lsm = jnp.where(strict, lsm, 0.0)
            hi = lsm.astype(BF16)
            lo = (lsm - hi.astype(F32)).astype(BF16)
            cs = (jnp.dot(hi, tri, preferred_element_type=F32)
                  + jnp.dot(lo, tri, preferred_element_type=F32))
            log_a = z + cs + r
            if diag:
                log_a = jnp.where(strict, log_a, NEG_INF)
            p = jnp.exp(log_a)
            acc = acc + jnp.dot(p.astype(BF16), vh[h], preferred_element_type=F32)
            new_r.append(r + cs[:, 0:1])
        return acc, new_r[0], new_r[1]

    zero_r = jnp.zeros((tq, 1), F32)
    carry = tile(i, (jnp.zeros((tq, LANES), F32), zero_r, zero_r), True)
    carry = lax.fori_loop(0, i, lambda jj, cr: tile(i - 1 - jj, cr, False), carry)
    acc = carry[0]
    sq = acc * acc
    ms0 = jnp.sum(jnp.where(first, sq, 0.0), axis=1, keepdims=True) * (1.0 / d)
    ms1 = jnp.sum(jnp.where(first, 0.0, sq), axis=1, keepdims=True) * (1.0 / d)
    ms = jnp.where(first, ms0, ms1)
    out_ref[...] = acc * lax.rsqrt(ms + EPS) * g_ref[...]


def _sb_attention(qkv_s, norm_g, B, S, tq=128):
    nq = S // tq
    npair = SB_WIDTH // LANES
    return pl.pallas_call(
        functools.partial(_sb_kernel, tq=tq),
        grid=(B, npair, nq),
        in_specs=[pl.BlockSpec((tq, LANES), lambda b, p, i: (b * nq + i, p)),
                  pl.BlockSpec((S, LANES), lambda b, p, i: (b, npair + p)),
                  pl.BlockSpec((S, LANES), lambda b, p, i: (b, 2 * npair + p)),
                  pl.BlockSpec((1, LANES), lambda b, p, i: (0, p))],
        out_specs=pl.BlockSpec((tq, LANES), lambda b, p, i: (b * nq + i, p)),
        out_shape=jax.ShapeDtypeStruct((B * S, SB_WIDTH), F32),
        compiler_params=_cparams("parallel", "parallel", "arbitrary"),
    )(qkv_s, qkv_s, qkv_s, norm_g.reshape(1, SB_WIDTH))


def _xattn_kernel(x_ref, hm_ref, hs_ref, wo1_ref, wo2_ref, g_ref, wq_ref, kv_ref, wo_ref, out_ref):
    D = x_ref.shape[1]
    dh = D // N_XATTN_HEADS
    x1 = (x_ref[...]
          + jnp.dot(hm_ref[...].astype(BF16), wo1_ref[...], preferred_element_type=F32)
          + jnp.dot(hs_ref[...].astype(BF16), wo2_ref[...], preferred_element_type=F32))
    xn = _rms(x1, g_ref[...])
    q = jnp.dot(xn.astype(BF16), wq_ref[...], preferred_element_type=F32)
    heads = []
    for h in range(N_XATTN_HEADS):
        qh = q[:, h * dh:(h + 1) * dh]
        kh = kv_ref[:, h * dh:(h + 1) * dh]
        vh = kv_ref[:, D + h * dh:D + (h + 1) * dh]
        s = _dot_nt(qh, kh) * (dh ** -0.5)
        s = s - jnp.max(s, axis=1, keepdims=True)
        p = jnp.exp(s)
        p = p / jnp.sum(p, axis=1, keepdims=True)
        heads.append(_dot(p, vh))
    o = jnp.concatenate(heads, axis=1)
    out_ref[...] = x1 + jnp.dot(o.astype(BF16), wo_ref[...], preferred_element_type=F32)


def _mixer_out_xattn(x, hm, hs, wo1, wo2, g, wq, kv, wo, B, S, M, tm=256):
    T, D = x.shape
    nt = S // tm
    row = lambda i: (i, 0)
    const = lambda i: (0, 0)
    return pl.pallas_call(
        _xattn_kernel,
        grid=(T // tm,),
        in_specs=[pl.BlockSpec((tm, D), row),
                  pl.BlockSpec((tm, MLSTM_WIDTH), row),
                  pl.BlockSpec((tm, SB_WIDTH), row),
                  pl.BlockSpec((MLSTM_WIDTH, D), const),
                  pl.BlockSpec((SB_WIDTH, D), const),
                  pl.BlockSpec((1, D), const),
                  pl.BlockSpec((D, D), const),
                  pl.BlockSpec((M, 2 * D), lambda i: (i // nt, 0)),
                  pl.BlockSpec((D, D), const)],
        out_specs=pl.BlockSpec((tm, D), row),
        out_shape=jax.ShapeDtypeStruct((T, D), F32),
        compiler_params=_cparams("parallel"),
    )(x, hm, hs, wo1, wo2, g.reshape(1, D), wq, kv, wo)


def _top_rows(s, ids, payload, k):
    big = jnp.int32(2 ** 30)
    vals, pays = [], []
    for _ in range(k):
        m = jnp.max(s, axis=0, keepdims=True)
        sel = jnp.min(jnp.where(s == m, ids, big), axis=0, keepdims=True)
        hit = ids == sel
        vals.append(m)
        pays.append(jnp.sum(jnp.where(hit, payload, 0), axis=0, keepdims=True))
        s = jnp.where(hit, NEG_INF, s)
    return jnp.concatenate(vals, axis=0), jnp.concatenate(pays, axis=0)


def _peer_route_kernel(q_ref, sk_ref, exp_ref, gate_ref):
    tm = q_ref.shape[0]
    K = PEER_TOPK
    key_ids = lax.broadcasted_iota(I32, (PEER_N_KEYS, tm), 0)
    exp_rows, gate_rows = [], []
    for h in range(PEER_HEADS):
        tops = []
        for p in range(2):
            hp = 2 * h + p
            qc = q_ref[:, hp * PEER_HALF:(hp + 1) * PEER_HALF]
            s = _dot_nt(sk_ref[hp], qc)
            tops.append(_top_rows(s, key_ids, key_ids, K))
        (v0, i0), (v1, i1) = tops
        cand, cpos, cexp = [], [], []
        for a in range(K):
            nb = K // (a + 1)
            cand.append(v0[a:a + 1, :] + v1[0:nb, :])
            cpos.append(a * K + lax.broadcasted_iota(I32, (nb, tm), 0))
            cexp.append(i0[a:a + 1, :] * PEER_N_KEYS + i1[0:nb, :])
        best, experts = _top_rows(jnp.concatenate(cand, axis=0), jnp.concatenate(cpos, axis=0),
                                  jnp.concatenate(cexp, axis=0), K)
        e = jnp.exp(best - best[0:1, :])
        gate_rows.append(e / jnp.sum(e, axis=0, keepdims=True))
        exp_rows.append(experts)
    exp_ref[0] = jnp.concatenate(exp_rows, axis=0)
    gate_ref[0] = jnp.concatenate(gate_rows, axis=0)


def _peer_route(q, subkeys, tm=128):
    T = q.shape[0]
    nb = T // tm
    return pl.pallas_call(
        _peer_route_kernel,
        grid=(nb,),
        in_specs=[pl.BlockSpec((tm, q.shape[1]), lambda i: (i, 0)),
                  pl.BlockSpec(subkeys.shape, lambda i: (0, 0, 0))],
        out_specs=[pl.BlockSpec((1, PEER_E, tm), lambda i: (i, 0, 0)),
                   pl.BlockSpec((1, PEER_E, tm), lambda i: (i, 0, 0))],
        out_shape=[jax.ShapeDtypeStruct((nb, PEER_E, tm), I32),
                   jax.ShapeDtypeStruct((nb, PEER_E, tm), F32)],
        compiler_params=_cparams("parallel"),
    )(q, subkeys)


SC_GATHER_WINDOW = 128


def _gather_rows(table, idx):
    N = idx.shape[0]
    W = table.shape[1]
    win = SC_GATHER_WINDOW
    mesh = plsc.VectorSubcoreMesh(core_axis_name="core", subcore_axis_name="subcore")
    n_cores = mesh.num_cores
    n_workers = n_cores * mesh.num_subcores
    assert N % (win * n_workers * 2) == 0
    nwin = N // (win * n_workers)

    @functools.partial(
        pl.kernel, out_type=jax.ShapeDtypeStruct((N, W), table.dtype), mesh=mesh,
        scratch_types=[pltpu.VMEM((nwin, win), I32),
                       pltpu.VMEM((win, W), table.dtype), pltpu.VMEM((win, W), table.dtype),
                       pltpu.SemaphoreType.DMA, pltpu.SemaphoreType.DMA,
                       pltpu.SemaphoreType.DMA, pltpu.SemaphoreType.DMA])
    def gather_kernel(table_hbm, idx_hbm, out_hbm, idx_v, buf0, buf1, gsem0, gsem1, wsem0, wsem1):
        wid = lax.axis_index("subcore") * n_cores + lax.axis_index("core")
        w0 = wid * nwin
        bufs, gsems, wsems = (buf0, buf1), (gsem0, gsem1), (wsem0, wsem1)
        pltpu.sync_copy(idx_hbm.at[pl.ds(w0, nwin)], idx_v)

        def gather(j, b):
            return pltpu.make_async_copy(table_hbm.at[idx_v.at[j]], bufs[b], gsems[b])

        def write(j, b):
            return pltpu.make_async_copy(bufs[b], out_hbm.at[pl.ds((w0 + j) * win, win)], wsems[b])

        gather(0, 0).start()

        @pl.loop(0, nwin, step=2)
        def _(j):
            for b in range(2):
                jj = j + b
                gather(jj, b).wait()
                write(jj, b).start()

                @pl.when(jj >= 1)
                def _():
                    write(jj - 1, 1 - b).wait()

                @pl.when(jj + 1 < nwin)
                def _():
                    gather(jj + 1, 1 - b).start()

        write(nwin - 1, 1).wait()

    return gather_kernel(table, idx.reshape(N // win, win))


PEER_TOKENS_PER_STEP = 8


def _unpack_pair(words):
    lo = lax.bitcast_convert_type(lax.shift_left(words, jnp.int32(16)), F32)
    hi = lax.bitcast_convert_type(jnp.bitwise_and(words, jnp.int32(-65536)), F32)
    return lo, hi


def _peer_ffn_kernel(x_ref, xn_ref, gate_ref, ua_ref, ub_ref, va_ref, vb_ref, fg_ref, out_ref):
    tb = x_ref.shape[0]
    E = PEER_E
    Q = xn_ref.shape[1] // 4
    tm = gate_ref.shape[2]
    lane0 = (pl.program_id(0) * tb) % tm
    G = gate_ref[0]
    lane = lax.broadcasted_iota(I32, G.shape, 1)
    rows = []
    for t in range(tb):
        xr = xn_ref[t:t + 1, :]
        gcol = jnp.sum(jnp.where(lane == lane0 + t, G, 0.0), axis=1, keepdims=True)
        u0, u1 = _unpack_pair(ua_ref[t * E:(t + 1) * E, :])
        u2, u3 = _unpack_pair(ub_ref[t * E:(t + 1) * E, :])
        prod = (u0 * xr[:, 0:Q] + u1 * xr[:, Q:2 * Q]) + (u2 * xr[:, 2 * Q:3 * Q] + u3 * xr[:, 3 * Q:])
        hcol = jnp.sum(prod, axis=1, keepdims=True)
        act = 0.5 * hcol * (1.0 + lax.erf(hcol * (2.0 ** -0.5))) * gcol
        v0, v1 = _unpack_pair(va_ref[t * E:(t + 1) * E, :])
        v2, v3 = _unpack_pair(vb_ref[t * E:(t + 1) * E, :])
        rows.append(jnp.concatenate([jnp.sum(act * vq, axis=0, keepdims=True) for vq in (v0, v1, v2, v3)],
                                    axis=1))
    y = jnp.concatenate(rows, axis=0)
    out_ref[...] = _rms(x_ref[...] + y, fg_ref[...])


def _peer_ffn(x2, xn, gates_t, ua, ub, va, vb, final_g, row0):
    tb = PEER_TOKENS_PER_STEP
    D = x2.shape[1]
    tm = gates_t.shape[2]
    Tc = ua.shape[0] // PEER_E
    assert row0 % tm == 0 and tm % tb == 0
    blk0 = row0 // tb
    rows_spec = pl.BlockSpec((tb * PEER_E, D // 4), lambda i: (i, 0))
    return pl.pallas_call(
        _peer_ffn_kernel,
        grid=(Tc // tb,),
        in_specs=[pl.BlockSpec((tb, D), lambda i: (blk0 + i, 0)),
                  pl.BlockSpec((tb, D), lambda i: (blk0 + i, 0)),
                  pl.BlockSpec((1, PEER_E, tm), lambda i: ((row0 + i * tb) // tm, 0, 0)),
                  rows_spec, rows_spec, rows_spec, rows_spec,
                  pl.BlockSpec((1, D), lambda i: (0, 0))],
        out_specs=pl.BlockSpec((tb, D), lambda i: (i, 0)),
        out_shape=jax.ShapeDtypeStruct((Tc, D), F32),
        compiler_params=_cparams("parallel"),
    )(x2, xn, gates_t, ua, ub, va, vb, final_g.reshape(1, D))


def _pack_bf16_pairs(w):
    Q = w.shape[1] // 4
    bits = lax.bitcast_convert_type(w.astype(BF16), jnp.uint16).astype(jnp.uint32)
    tables = []
    for s in range(2):
        words = bits[:, 2 * s * Q:(2 * s + 1) * Q] | (bits[:, (2 * s + 1) * Q:(2 * s + 2) * Q] << 16)
        tables.append(lax.bitcast_convert_type(words, I32))
    return tables


PEER_TOKEN_CHUNK = 4096


def kernel(x, mem, mix_norm_g, w_in, conv_w, conv_b, igate_b, fgate_b, mlstm_norm_g, sb_norm_g, w_out, xattn_norm_g, mem_norm_g, xattn_wq, xattn_wkv, xattn_wo, ffn_norm_g, peer_wq, peer_subkeys, peer_u, peer_v, final_norm_g):
    B, S, D = x.shape
    M = mem.shape[1]
    T = B * S
    depth = w_in.shape[0]
    assert depth == 1, "the output norm is fused into the last PEER kernel; only one layer is supported"
    MW = MLSTM_WIDTH
    xt = x.reshape(T, D)
    for l in range(depth):
        w = w_in[l]
        gate_cols = jnp.pad(w[:, 4 * MW:4 * MW + 2 * N_MLSTM_HEADS], ((0, 0), (0, LANES - 2 * N_MLSTM_HEADS)))
        w_all = jnp.concatenate([w[:, :4 * MW], gate_cols, w[:, 4 * MW + 2 * N_MLSTM_HEADS:]], axis=1).astype(BF16)
        qk_pre, v_m, o_m, gates, qkv_s = _norm_proj(
            xt, mix_norm_g[l], w_all, (2 * MW, MW, MW, LANES, 3 * SB_WIDTH))
        gate_b = jnp.pad(jnp.concatenate([igate_b[l], fgate_b[l]]), (0, LANES - 2 * N_MLSTM_HEADS))
        h_m = _mlstm(qk_pre, v_m, o_m, gates, conv_w[l], conv_b[l], gate_b, mlstm_norm_g[l], B, S)
        h_s = _sb_attention(qkv_s, sb_norm_g[l], B, S)

        (kv,) = _norm_proj(mem.reshape(B * M, D), mem_norm_g[l], xattn_wkv[l].astype(BF16), (2 * D,))
        wo = w_out[l].astype(BF16)
        x2 = _mixer_out_xattn(xt, h_m, h_s, wo[:MW], wo[MW:], xattn_norm_g[l], xattn_wq[l].astype(BF16),
                              kv, xattn_wo[l].astype(BF16), B, S, M)

        xn3, q_peer = _norm_proj(x2, ffn_norm_g[l], peer_wq[l].astype(BF16),
                                 (PEER_HEADS * 2 * PEER_HALF,), emit_norm=True)
        subkeys = peer_subkeys[l].reshape(PEER_HEADS * 2, PEER_N_KEYS, PEER_HALF).astype(BF16)
        experts_t, gates_t = _peer_route(q_peer, subkeys)
        E = PEER_E
        experts = experts_t.transpose(0, 2, 1).reshape(T * E)

        ua_t, ub_t = _pack_bf16_pairs(peer_u[l])
        va_t, vb_t = _pack_bf16_pairs(peer_v[l])
        Tc = min(PEER_TOKEN_CHUNK, T)
        outs = []
        for c in range(T // Tc):
            idx_c = lax.slice(experts, (c * Tc * E,), ((c + 1) * Tc * E,))
            rows = [_gather_rows(tbl, idx_c) for tbl in (ua_t, ub_t, va_t, vb_t)]
            outs.append(_peer_ffn(x2, xn3, gates_t, *rows, final_norm_g, c * Tc))
        xt = jnp.concatenate(outs, axis=0) if len(outs) > 1 else outs[0]
    return xt.reshape(B, S, D)
```

```python
import functools
import math

import jax
import jax.numpy as jnp
from jax import lax
from jax.experimental import pallas as pl
from jax.experimental.pallas import tpu as pltpu
from jax.experimental.pallas import tpu_sc as plsc

F32 = jnp.float32
BF16 = jnp.bfloat16
I32 = jnp.int32

EPS = 1e-6
LANES = 128
N_MLSTM_HEADS = 4
MLSTM_HEAD_DIM = 128
MLSTM_WIDTH = N_MLSTM_HEADS * MLSTM_HEAD_DIM
N_SB_HEADS = 8
SB_HEAD_DIM = 64
SB_WIDTH = N_SB_HEADS * SB_HEAD_DIM
CONV_WIDTH = 4
MLSTM_CHUNK = 128
N_XATTN_HEADS = 4
PEER_HEADS = 8
PEER_N_KEYS = 128
PEER_TOPK = 16
PEER_HALF = 128
PEER_E = PEER_HEADS * PEER_TOPK
VMEM_LIMIT = 48 * 1024 * 1024
NEG_INF = float("-inf")


def _cparams(*sem):
    return pltpu.CompilerParams(dimension_semantics=sem, vmem_limit_bytes=VMEM_LIMIT)


def _rms(x, g):
    return x * lax.rsqrt(jnp.mean(x * x, axis=-1, keepdims=True) + EPS) * g


def _sigmoid(x):
    return 1.0 / (1.0 + jnp.exp(-x))


def _log_sigmoid(x):
    return jnp.minimum(x, 0.0) - jnp.log(1.0 + jnp.exp(-jnp.abs(x)))


def _dot(a, b):
    return jnp.dot(a.astype(BF16), b.astype(BF16), preferred_element_type=F32)


def _dot_nt(a, b):
    return lax.dot_general(a.astype(BF16), b.astype(BF16), (((1,), (1,)), ((), ())),
                           preferred_element_type=F32)


def _dot_tn(a, b):
    return lax.dot_general(a.astype(BF16), b.astype(BF16), (((0,), (0,)), ((), ())),
                           preferred_element_type=F32)


def _norm_proj_kernel(x_ref, g_ref, w_ref, *out_refs, splits, emit_norm):
    xn = _rms(x_ref[...], g_ref[...])
    y = jnp.dot(xn.astype(BF16), w_ref[...], preferred_element_type=F32)
    outs = out_refs
    if emit_norm:
        outs[0][...] = xn
        outs = outs[1:]
    off = 0
    for o_ref, width in zip(outs, splits):
        o_ref[...] = y[:, off:off + width]
        off += width


def _norm_proj(x, g, w, splits, emit_norm=False, tm=256):
    T, D = x.shape
    N = w.shape[1]
    assert sum(splits) == N and T % tm == 0
    out_shape = [jax.ShapeDtypeStruct((T, wd), F32) for wd in splits]
    out_specs = [pl.BlockSpec((tm, wd), lambda i: (i, 0)) for wd in splits]
    if emit_norm:
        out_shape = [jax.ShapeDtypeStruct((T, D), F32)] + out_shape
        out_specs = [pl.BlockSpec((tm, D), lambda i: (i, 0))] + out_specs
    return pl.pallas_call(
        functools.partial(_norm_proj_kernel, splits=tuple(splits), emit_norm=emit_norm),
        grid=(T // tm,),
        in_specs=[pl.BlockSpec((tm, D), lambda i: (i, 0)),
                  pl.BlockSpec((1, D), lambda i: (0, 0)),
                  pl.BlockSpec((D, N), lambda i: (0, 0))],
        out_specs=out_specs,
        out_shape=out_shape,
        compiler_params=_cparams("parallel"),
    )(x, g.reshape(1, D), w)


def _mlstm_kernel(qk_ref, v_ref, o_ref, gt_ref, cw_ref, cb_ref, gb_ref, ng_ref, out_ref,
                  ext_ref, ct_ref, n_ref, m_ref):
    L = MLSTM_CHUNK
    dh = MLSTM_HEAD_DIM
    c = pl.program_id(1)

    @pl.when(c == 0)
    def _():
        ext_ref[0:8, :] = jnp.zeros((8, ext_ref.shape[1]), F32)
        ct_ref[...] = jnp.zeros(ct_ref.shape, F32)
        n_ref[...] = jnp.zeros(n_ref.shape, F32)
        m_ref[...] = jnp.zeros(m_ref.shape, F32)

    a = qk_ref[...]
    ext_ref[8:8 + L, :] = a
    conv = cb_ref[...]
    for j in range(CONV_WIDTH):
        conv = conv + cw_ref[j:j + 1, :] * ext_ref[pl.ds(8 - (CONV_WIDTH - 1) + j, L), :]
    ext_ref[0:8, :] = a[L - 8:L, :]
    qk = conv * _sigmoid(conv)

    G = gt_ref[...] + gb_ref[...]
    lane = lax.broadcasted_iota(I32, G.shape, 1)
    LG = jnp.where(lane < N_MLSTM_HEADS, G, _log_sigmoid(G))
    rr = lax.broadcasted_iota(I32, (L, L), 0)
    cc = lax.broadcasted_iota(I32, (L, L), 1)
    causal = cc <= rr
    tril = causal.astype(F32)
    bcum = jnp.dot(tril, LG, preferred_element_type=F32, precision=lax.Precision.HIGHEST)
    LGT = LG.T
    bcumT = bcum.T

    for h in range(N_MLSTM_HEADS):
        q = qk[:, h * dh:(h + 1) * dh]
        k = qk[:, MLSTM_WIDTH + h * dh:MLSTM_WIDTH + (h + 1) * dh] * (dh ** -0.5)
        v = v_ref[:, h * dh:(h + 1) * dh]
        i_col = LG[:, h:h + 1]
        i_row = LGT[h:h + 1, :]
        b_col = bcum[:, N_MLSTM_HEADS + h:N_MLSTM_HEADS + h + 1]
        b_row = bcumT[N_MLSTM_HEADS + h:N_MLSTM_HEADS + h + 1, :]
        m_prev = m_ref[h:h + 1, 0:1]
        ct = ct_ref[h]
        n_row = n_ref[h:h + 1, :]

        g_col = b_col + m_prev
        Dm = jnp.where(causal, b_col - b_row + i_row, NEG_INF)
        m_t = jnp.maximum(g_col, jnp.max(Dm, axis=1, keepdims=True))
        W = jnp.exp(Dm - m_t)
        inter = jnp.exp(g_col - m_t)
        s = _dot_nt(q, k) * W
        num = inter * _dot(q, ct) + _dot(s, v)
        den = inter * jnp.sum(q * n_row, axis=1, keepdims=True) + jnp.sum(s, axis=1, keepdims=True)
        hh = num / jnp.maximum(jnp.abs(den), jnp.exp(-m_t))

        F = b_col[L - 1:L, :]
        a_col = F - b_col + i_col
        m_new = jnp.maximum(F + m_prev, jnp.max(a_col, axis=0, keepdims=True))
        decay = jnp.exp(F + m_prev - m_new)
        w_col = jnp.exp(a_col - m_new)
        ct_ref[h] = decay * ct + _dot_tn(k, v * w_col)
        n_ref[h:h + 1, :] = decay * n_row + jnp.sum(k * w_col, axis=0, keepdims=True)
        m_ref[h:h + 1, :] = jnp.broadcast_to(m_new, (1, m_ref.shape[1]))

        hh = _sigmoid(o_ref[:, h * dh:(h + 1) * dh]) * hh
        hh = hh * lax.rsqrt(jnp.mean(hh * hh, axis=1, keepdims=True) + EPS)
        out_ref[:, h * dh:(h + 1) * dh] = hh * ng_ref[:, h * dh:(h + 1) * dh]


def _mlstm(qk_pre, v_m, o_m, gates, conv_w, conv_b, gate_b, norm_g, B, S):
    L = MLSTM_CHUNK
    nc = S // L
    W2 = 2 * MLSTM_WIDTH
    row = lambda b, c: (b * nc + c, 0)
    const = lambda b, c: (0, 0)
    return pl.pallas_call(
        _mlstm_kernel,
        grid=(B, nc),
        in_specs=[pl.BlockSpec((L, W2), row),
                  pl.BlockSpec((L, MLSTM_WIDTH), row),
                  pl.BlockSpec((L, MLSTM_WIDTH), row),
                  pl.BlockSpec((L, LANES), row),
                  pl.BlockSpec((CONV_WIDTH, W2), const),
                  pl.BlockSpec((1, W2), const),
                  pl.BlockSpec((1, LANES), const),
                  pl.BlockSpec((1, MLSTM_WIDTH), const)],
        out_specs=pl.BlockSpec((L, MLSTM_WIDTH), row),
        out_shape=jax.ShapeDtypeStruct((B * S, MLSTM_WIDTH), F32),
        scratch_shapes=[pltpu.VMEM((8 + L, W2), F32),
                        pltpu.VMEM((N_MLSTM_HEADS, MLSTM_HEAD_DIM, MLSTM_HEAD_DIM), F32),
                        pltpu.VMEM((8, MLSTM_HEAD_DIM), F32),
                        pltpu.VMEM((8, LANES), F32)],
        compiler_params=_cparams("parallel", "arbitrary"),
    )(qk_pre, v_m, o_m, gates, conv_w, conv_b.reshape(1, W2), gate_b.reshape(1, LANES),
      norm_g.reshape(1, MLSTM_WIDTH))


SB_Q_TILE = 512
SB_K_TILE = 128


def _sb_kernel(q_ref, k_ref, v_ref, g_ref, out_ref, q_ref_bf, acc_ref, rb_ref, *, tq, tk):
    i = pl.program_id(2)
    d = SB_HEAD_DIM
    nsub = tq // tk
    lane = lax.broadcasted_iota(I32, (1, LANES), 1)
    first = lane < d
    q_ref_bf[...] = (q_ref[...] * (d ** -0.5)).astype(BF16)
    acc_ref[...] = jnp.zeros(acc_ref.shape, F32)
    rb_ref[...] = jnp.zeros(rb_ref.shape, F32)
    rr = lax.broadcasted_iota(I32, (tk, 2 * tk), 0)
    cc = lax.broadcasted_iota(I32, (tk, 2 * tk), 1)
    tri_ones = jnp.where((rr >= cc) | (cc >= tk), 1.0, 0.0).astype(BF16)

    def tile(j, masked):
        start = pl.multiple_of(j * tk, tk)
        kb = k_ref[pl.ds(start, tk), :]
        vb = v_ref[pl.ds(start, tk), :]
        k2 = jnp.concatenate([jnp.where(first, kb, 0.0), jnp.where(first, 0.0, kb)], axis=0).astype(BF16)
        v2 = jnp.concatenate([jnp.where(first, vb, 0.0), jnp.where(first, 0.0, vb)], axis=0).astype(BF16)
        if masked:
            strict = (j * tk + lax.broadcasted_iota(I32, (tq, tk), 1)
                      < i * tq + lax.broadcasted_iota(I32, (tq, tk), 0))
        z2 = lax.dot_general(q_ref_bf[...], k2, (((1,), (1,)), ((), ())), preferred_element_type=F32)
        ps = []
        for h in range(2):
            z = z2[:, h * tk:(h + 1) * tk]
            lsm = -(jnp.maximum(z, 0.0) + jnp.log(1.0 + jnp.exp(-jnp.abs(z))))
            if masked:
                lsm = jnp.where(strict, lsm, 0.0)
            hi = lsm.astype(BF16)
            lo = (lsm - hi.astype(F32)).astype(BF16)
            cs2 = (jnp.dot(hi, tri_ones, preferred_element_type=F32)
                   + jnp.dot(lo, tri_ones, preferred_element_type=F32))
            rb = rb_ref[h]
            log_a = z + cs2[:, :tk] + rb
            if masked:
                log_a = jnp.where(strict, log_a, NEG_INF)
            ps.append(jnp.exp(log_a).astype(BF16))
            rb_ref[h] = rb + cs2[:, tk:]
        acc_ref[...] += jnp.dot(jnp.concatenate(ps, axis=1), v2, preferred_element_type=F32)

    def masked_step(jj, carry):
        tile(i * nsub + nsub - 1 - jj, True)
        return carry

    def plain_step(jj, carry):
        tile(i * nsub - 1 - jj, False)
        return carry

    lax.fori_loop(0, nsub, masked_step, 0)
    lax.fori_loop(0, i * nsub, plain_step, 0)
    acc = acc_ref[...]
    sq = acc * acc
    ms0 = jnp.sum(jnp.where(first, sq, 0.0), axis=1, keepdims=True) * (1.0 / d)
    ms1 = jnp.sum(jnp.where(first, 0.0, sq), axis=1, keepdims=True) * (1.0 / d)
    ms = jnp.where(first, ms0, ms1)
    out_ref[...] = acc * lax.rsqrt(ms + EPS) * g_ref[...]


def _sb_attention(qkv_s, norm_g, B, S):
    tq = min(SB_Q_TILE, S)
    tk = SB_K_TILE
    nq = S // tq
    npair = SB_WIDTH // LANES
    return pl.pallas_call(
        functools.partial(_sb_kernel, tq=tq, tk=tk),
        grid=(B, npair, nq),
        in_specs=[pl.BlockSpec((tq, LANES), lambda b, p, i: (b * nq + i, p)),
                  pl.BlockSpec((S, LANES), lambda b, p, i: (b, npair + p)),
                  pl.BlockSpec((S, LANES), lambda b, p, i: (b, 2 * npair + p)),
                  pl.BlockSpec((1, LANES), lambda b, p, i: (0, p))],
        out_specs=pl.BlockSpec((tq, LANES), lambda b, p, i: (b * nq + i, p)),
        out_shape=jax.ShapeDtypeStruct((B * S, SB_WIDTH), F32),
        scratch_shapes=[pltpu.VMEM((tq, LANES), BF16),
                        pltpu.VMEM((tq, LANES), F32),
                        pltpu.VMEM((2, tq, LANES), F32)],
        compiler_params=_cparams("parallel", "parallel", "arbitrary"),
    )(qkv_s, qkv_s, qkv_s, norm_g.reshape(1, SB_WIDTH))


def _xattn_kernel(x_ref, hm_ref, hs_ref, wo1_ref, wo2_ref, g_ref, wq_ref, kv_ref, wo_ref, out_ref):
    D = x_ref.shape[1]
    dh = D // N_XATTN_HEADS
    x1 = (x_ref[...]
          + jnp.dot(hm_ref[...].astype(BF16), wo1_ref[...], preferred_element_type=F32)
          + jnp.dot(hs_ref[...].astype(BF16), wo2_ref[...], preferred_element_type=F32))
    xn = _rms(x1, g_ref[...])
    q = jnp.dot(xn.astype(BF16), wq_ref[...], preferred_element_type=F32)
    heads = []
    for h in range(N_XATTN_HEADS):
        qh = q[:, h * dh:(h + 1) * dh]
        kh = kv_ref[:, h * dh:(h + 1) * dh]
        vh = kv_ref[:, D + h * dh:D + (h + 1) * dh]
        s = _dot_nt(qh, kh) * (dh ** -0.5)
        s = s - jnp.max(s, axis=1, keepdims=True)
        p = jnp.exp(s)
        p = p / jnp.sum(p, axis=1, keepdims=True)
        heads.append(_dot(p, vh))
    o = jnp.concatenate(heads, axis=1)
    out_ref[...] = x1 + jnp.dot(o.astype(BF16), wo_ref[...], preferred_element_type=F32)


def _mixer_out_xattn(x, hm, hs, wo1, wo2, g, wq, kv, wo, B, S, M, tm=256):
    T, D = x.shape
    nt = S // tm
    row = lambda i: (i, 0)
    const = lambda i: (0, 0)
    return pl.pallas_call(
        _xattn_kernel,
        grid=(T // tm,),
        in_specs=[pl.BlockSpec((tm, D), row),
                  pl.BlockSpec((tm, MLSTM_WIDTH), row),
                  pl.BlockSpec((tm, SB_WIDTH), row),
                  pl.BlockSpec((MLSTM_WIDTH, D), const),
                  pl.BlockSpec((SB_WIDTH, D), const),
                  pl.BlockSpec((1, D), const),
                  pl.BlockSpec((D, D), const),
                  pl.BlockSpec((M, 2 * D), lambda i: (i // nt, 0)),
                  pl.BlockSpec((D, D), const)],
        out_specs=pl.BlockSpec((tm, D), row),
        out_shape=jax.ShapeDtypeStruct((T, D), F32),
        compiler_params=_cparams("parallel"),
    )(x, hm, hs, wo1, wo2, g.reshape(1, D), wq, kv, wo)


def _top_rows(s, ids, payload, k):
    big = jnp.int32(2 ** 30)
    vals, pays = [], []
    for _ in range(k):
        m = jnp.max(s, axis=0, keepdims=True)
        sel = jnp.min(jnp.where(s == m, ids, big), axis=0, keepdims=True)
        hit = ids == sel
        vals.append(m)
        pays.append(jnp.sum(jnp.where(hit, payload, 0), axis=0, keepdims=True))
        s = jnp.where(hit, NEG_INF, s)
    return jnp.concatenate(vals, axis=0), jnp.concatenate(pays, axis=0)


def _peer_route_kernel(q_ref, sk_ref, exp_ref, gate_ref):
    tm = q_ref.shape[0]
    K = PEER_TOPK
    key_ids = lax.broadcasted_iota(I32, (PEER_N_KEYS, tm), 0)
    exp_rows, gate_rows = [], []
    for h in range(PEER_HEADS):
        tops = []
        for p in range(2):
            hp = 2 * h + p
            qc = q_ref[:, hp * PEER_HALF:(hp + 1) * PEER_HALF]
            s = _dot_nt(sk_ref[hp], qc)
            tops.append(_top_rows(s, key_ids, key_ids, K))
        (v0, i0), (v1, i1) = tops
        cand, cpos, cexp = [], [], []
        for a in range(K):
            nb = K // (a + 1)
            cand.append(v0[a:a + 1, :] + v1[0:nb, :])
            cpos.append(a * K + lax.broadcasted_iota(I32, (nb, tm), 0))
            cexp.append(i0[a:a + 1, :] * PEER_N_KEYS + i1[0:nb, :])
        best, experts = _top_rows(jnp.concatenate(cand, axis=0), jnp.concatenate(cpos, axis=0),
                                  jnp.concatenate(cexp, axis=0), K)
        e = jnp.exp(best - best[0:1, :])
        gate_rows.append(e / jnp.sum(e, axis=0, keepdims=True))
        exp_rows.append(experts)
    exp_ref[0] = jnp.concatenate(exp_rows, axis=0)
    gate_ref[0] = jnp.concatenate(gate_rows, axis=0)


def _peer_route(q, subkeys, tm=128):
    T = q.shape[0]
    nb = T // tm
    return pl.pallas_call(
        _peer_route_kernel,
        grid=(nb,),
        in_specs=[pl.BlockSpec((tm, q.shape[1]), lambda i: (i, 0)),
                  pl.BlockSpec(subkeys.shape, lambda i: (0, 0, 0))],
        out_specs=[pl.BlockSpec((1, PEER_E, tm), lambda i: (i, 0, 0)),
                   pl.BlockSpec((1, PEER_E, tm), lambda i: (i, 0, 0))],
        out_shape=[jax.ShapeDtypeStruct((nb, PEER_E, tm), I32),
                   jax.ShapeDtypeStruct((nb, PEER_E, tm), F32)],
        compiler_params=_cparams("parallel"),
    )(q, subkeys)


SC_GATHER_WINDOW = 128


def _gather_rows(table, idx):
    N = idx.shape[0]
    W = table.shape[1]
    win = SC_GATHER_WINDOW
    mesh = plsc.VectorSubcoreMesh(core_axis_name="core", subcore_axis_name="subcore")
    n_cores = mesh.num_cores
    n_workers = n_cores * mesh.num_subcores
    assert N % (win * n_workers * 2) == 0
    nwin = N // (win * n_workers)

    @functools.partial(
        pl.kernel, out_type=jax.ShapeDtypeStruct((N, W), table.dtype), mesh=mesh,
        scratch_types=[pltpu.VMEM((nwin, win), I32),
                       pltpu.VMEM((win, W), table.dtype), pltpu.VMEM((win, W), table.dtype),
                       pltpu.SemaphoreType.DMA, pltpu.SemaphoreType.DMA,
                       pltpu.SemaphoreType.DMA, pltpu.SemaphoreType.DMA])
    def gather_kernel(table_hbm, idx_hbm, out_hbm, idx_v, buf0, buf1, gsem0, gsem1, wsem0, wsem1):
        wid = lax.axis_index("subcore") * n_cores + lax.axis_index("core")
        w0 = wid * nwin
        bufs, gsems, wsems = (buf0, buf1), (gsem0, gsem1), (wsem0, wsem1)
        pltpu.sync_copy(idx_hbm.at[pl.ds(w0, nwin)], idx_v)

        def gather(j, b):
            return pltpu.make_async_copy(table_hbm.at[idx_v.at[j]], bufs[b], gsems[b])

        def write(j, b):
            return pltpu.make_async_copy(bufs[b], out_hbm.at[pl.ds((w0 + j) * win, win)], wsems[b])

        gather(0, 0).start()

        @pl.loop(0, nwin, step=2)
        def _(j):
            for b in range(2):
                jj = j + b
                gather(jj, b).wait()
                write(jj, b).start()

                @pl.when(jj >= 1)
                def _():
                    write(jj - 1, 1 - b).wait()

                @pl.when(jj + 1 < nwin)
                def _():
                    gather(jj + 1, 1 - b).start()

        write(nwin - 1, 1).wait()

    return gather_kernel(table, idx.reshape(N // win, win))


PEER_TOKENS_PER_STEP = 8


def _unpack_pair(words):
    lo = lax.bitcast_convert_type(lax.shift_left(words, jnp.int32(16)), F32)
    hi = lax.bitcast_convert_type(jnp.bitwise_and(words, jnp.int32(-65536)), F32)
    return lo, hi


def _peer_ffn_kernel(x_ref, xn_ref, gate_ref, ua_ref, ub_ref, va_ref, vb_ref, fg_ref, out_ref):
    tb = x_ref.shape[0]
    E = PEER_E
    Q = xn_ref.shape[1] // 4
    tm = gate_ref.shape[2]
    lane0 = (pl.program_id(0) * tb) % tm
    G = gate_ref[0]
    lane = lax.broadcasted_iota(I32, G.shape, 1)
    rows = []
    for t in range(tb):
        xr = xn_ref[t:t + 1, :]
        gcol = jnp.sum(jnp.where(lane == lane0 + t, G, 0.0), axis=1, keepdims=True)
        u0, u1 = _unpack_pair(ua_ref[t * E:(t + 1) * E, :])
        u2, u3 = _unpack_pair(ub_ref[t * E:(t + 1) * E, :])
        prod = (u0 * xr[:, 0:Q] + u1 * xr[:, Q:2 * Q]) + (u2 * xr[:, 2 * Q:3 * Q] + u3 * xr[:, 3 * Q:])
        hcol = jnp.sum(prod, axis=1, keepdims=True)
        act = 0.5 * hcol * (1.0 + lax.erf(hcol * (2.0 ** -0.5))) * gcol
        v0, v1 = _unpack_pair(va_ref[t * E:(t + 1) * E, :])
        v2, v3 = _unpack_pair(vb_ref[t * E:(t + 1) * E, :])
        rows.append(jnp.concatenate([jnp.sum(act * vq, axis=0, keepdims=True) for vq in (v0, v1, v2, v3)],
                                    axis=1))
    y = jnp.concatenate(rows, axis=0)
    out_ref[...] = _rms(x_ref[...] + y, fg_ref[...])


def _peer_ffn(x2, xn, gates_t, ua, ub, va, vb, final_g, row0):
    tb = PEER_TOKENS_PER_STEP
    D = x2.shape[1]
    tm = gates_t.shape[2]
    Tc = ua.shape[0] // PEER_E
    assert row0 % tm == 0 and tm % tb == 0
    blk0 = row0 // tb
    rows_spec = pl.BlockSpec((tb * PEER_E, D // 4), lambda i: (i, 0))
    return pl.pallas_call(
        _peer_ffn_kernel,
        grid=(Tc // tb,),
        in_specs=[pl.BlockSpec((tb, D), lambda i: (blk0 + i, 0)),
                  pl.BlockSpec((tb, D), lambda i: (blk0 + i, 0)),
                  pl.BlockSpec((1, PEER_E, tm), lambda i: ((row0 + i * tb) // tm, 0, 0)),
                  rows_spec, rows_spec, rows_spec, rows_spec,
                  pl.BlockSpec((1, D), lambda i: (0, 0))],
        out_specs=pl.BlockSpec((tb, D), lambda i: (i, 0)),
        out_shape=jax.ShapeDtypeStruct((Tc, D), F32),
        compiler_params=_cparams("parallel"),
    )(x2, xn, gates_t, ua, ub, va, vb, final_g.reshape(1, D))


def _pack_bf16_pairs(w):
    Q = w.shape[1] // 4
    bits = lax.bitcast_convert_type(w.astype(BF16), jnp.uint16).astype(jnp.uint32)
    tables = []
    for s in range(2):
        words = bits[:, 2 * s * Q:(2 * s + 1) * Q] | (bits[:, (2 * s + 1) * Q:(2 * s + 2) * Q] << 16)
        tables.append(lax.bitcast_convert_type(words, I32))
    return tables


PEER_TOKEN_CHUNK = 4096
BATCH_GROUPS = 4


def kernel(x, mem, mix_norm_g, w_in, conv_w, conv_b, igate_b, fgate_b, mlstm_norm_g, sb_norm_g, w_out, xattn_norm_g, mem_norm_g, xattn_wq, xattn_wkv, xattn_wo, ffn_norm_g, peer_wq, peer_subkeys, peer_u, peer_v, final_norm_g):
    B, S, D = x.shape
    M = mem.shape[1]
    assert w_in.shape[0] == 1, "the output norm is fused into the last PEER kernel; only one layer is supported"
    MW = MLSTM_WIDTH
    E = PEER_E
    n_groups = BATCH_GROUPS if B % BATCH_GROUPS == 0 else 1
    Bg = B // n_groups
    Tg = Bg * S

    w = w_in[0]
    gate_cols = jnp.pad(w[:, 4 * MW:4 * MW + 2 * N_MLSTM_HEADS], ((0, 0), (0, LANES - 2 * N_MLSTM_HEADS)))
    w_all = jnp.concatenate([w[:, :4 * MW], gate_cols, w[:, 4 * MW + 2 * N_MLSTM_HEADS:]], axis=1).astype(BF16)
    gate_b = jnp.pad(jnp.concatenate([igate_b[0], fgate_b[0]]), (0, LANES - 2 * N_MLSTM_HEADS))
    wo = w_out[0].astype(BF16)
    wkv = xattn_wkv[0].astype(BF16)
    wq = xattn_wq[0].astype(BF16)
    wxo = xattn_wo[0].astype(BF16)
    wpq = peer_wq[0].astype(BF16)
    subkeys = peer_subkeys[0].reshape(PEER_HEADS * 2, PEER_N_KEYS, PEER_HALF).astype(BF16)
    tables = _pack_bf16_pairs(peer_u[0]) + _pack_bf16_pairs(peer_v[0])
    Tc = min(PEER_TOKEN_CHUNK, Tg)

    def dense_and_route(g):
        xt = x[g * Bg:(g + 1) * Bg].reshape(Tg, D)
        qk_pre, v_m, o_m, gates, qkv_s = _norm_proj(
            xt, mix_norm_g[0], w_all, (2 * MW, MW, MW, LANES, 3 * SB_WIDTH))
        h_m = _mlstm(qk_pre, v_m, o_m, gates, conv_w[0], conv_b[0], gate_b, mlstm_norm_g[0], Bg, S)
        h_s = _sb_attention(qkv_s, sb_norm_g[0], Bg, S)
        (kv,) = _norm_proj(mem[g * Bg:(g + 1) * Bg].reshape(Bg * M, D), mem_norm_g[0], wkv, (2 * D,))
        x2 = _mixer_out_xattn(xt, h_m, h_s, wo[:MW], wo[MW:], xattn_norm_g[0], wq, kv, wxo, Bg, S, M)
        xn3, q_peer = _norm_proj(x2, ffn_norm_g[0], wpq, (PEER_HEADS * 2 * PEER_HALF,), emit_norm=True)
        experts_t, gates_t = _peer_route(q_peer, subkeys)
        experts = experts_t.transpose(0, 2, 1).reshape(Tg * E)
        rows = []
        for c in range(Tg // Tc):
            idx_c = lax.slice(experts, (c * Tc * E,), ((c + 1) * Tc * E,))
            rows.append([_gather_rows(tbl, idx_c) for tbl in tables])
        return x2, xn3, gates_t, rows

    def expert_ffn(state):
        x2, xn3, gates_t, rows = state
        return [_peer_ffn(x2, xn3, gates_t, *rows_c, final_norm_g, c * Tc) for c, rows_c in enumerate(rows)]

    outs = []
    prev = None
    for g in range(n_groups):
        state = dense_and_route(g)
        if prev is not None:
            outs += expert_ffn(prev)
        prev = state
    outs += expert_ffn(prev)
    out = jnp.concatenate(outs, axis=0) if len(outs) > 1 else outs[0]
    return out.reshape(B, S, D)
```

```python
import dataclasses
import functools

import jax
import jax.numpy as jnp
from jax import lax
from jax.experimental import pallas as pl
from jax.experimental.pallas import tpu as pltpu
from jax.experimental.pallas import tpu_sc as plsc

F32 = jnp.float32
BF16 = jnp.bfloat16
I32 = jnp.int32

EPS = 1e-6
LANES = 128
N_MLSTM_HEADS = 4
MLSTM_HEAD_DIM = 128
MLSTM_WIDTH = N_MLSTM_HEADS * MLSTM_HEAD_DIM
N_SB_HEADS = 8
SB_HEAD_DIM = 64
SB_WIDTH = N_SB_HEADS * SB_HEAD_DIM
CONV_WIDTH = 4
MLSTM_CHUNK = 128
N_XATTN_HEADS = 4
PEER_HEADS = 8
PEER_N_KEYS = 128
PEER_TOPK = 16
PEER_HALF = 128
PEER_E = PEER_HEADS * PEER_TOPK
VMEM_LIMIT = 48 * 1024 * 1024
NEG_INF = float("-inf")


def _cparams(*sem):
    return pltpu.CompilerParams(dimension_semantics=sem, vmem_limit_bytes=VMEM_LIMIT)


def _rms(x, g):
    return x * lax.rsqrt(jnp.mean(x * x, axis=-1, keepdims=True) + EPS) * g


def _sigmoid(x):
    return 1.0 / (1.0 + jnp.exp(-x))


def _log_sigmoid(x):
    return jnp.minimum(x, 0.0) - jnp.log(1.0 + jnp.exp(-jnp.abs(x)))


def _dot(a, b):
    return jnp.dot(a.astype(BF16), b.astype(BF16), preferred_element_type=F32)


def _dot_nt(a, b):
    return lax.dot_general(a.astype(BF16), b.astype(BF16), (((1,), (1,)), ((), ())),
                           preferred_element_type=F32)


def _dot_tn(a, b):
    return lax.dot_general(a.astype(BF16), b.astype(BF16), (((0,), (0,)), ((), ())),
                           preferred_element_type=F32)


def _norm_proj_kernel(x_ref, g_ref, w_ref, *out_refs, splits, emit_norm):
    xn = _rms(x_ref[...], g_ref[...])
    y = jnp.dot(xn.astype(BF16), w_ref[...], preferred_element_type=F32)
    outs = out_refs
    if emit_norm:
        outs[0][...] = xn
        outs = outs[1:]
    off = 0
    for o_ref, width in zip(outs, splits):
        o_ref[...] = y[:, off:off + width]
        off += width


def _norm_proj(x, g, w, splits, emit_norm=False, tm=256):
    T, D = x.shape
    N = w.shape[1]
    assert sum(splits) == N and T % tm == 0
    out_shape = [jax.ShapeDtypeStruct((T, wd), F32) for wd in splits]
    out_specs = [pl.BlockSpec((tm, wd), lambda i: (i, 0)) for wd in splits]
    if emit_norm:
        out_shape = [jax.ShapeDtypeStruct((T, D), F32)] + out_shape
        out_specs = [pl.BlockSpec((tm, D), lambda i: (i, 0))] + out_specs
    return pl.pallas_call(
        functools.partial(_norm_proj_kernel, splits=tuple(splits), emit_norm=emit_norm),
        grid=(T // tm,),
        in_specs=[pl.BlockSpec((tm, D), lambda i: (i, 0)),
                  pl.BlockSpec((1, D), lambda i: (0, 0)),
                  pl.BlockSpec((D, N), lambda i: (0, 0))],
        out_specs=out_specs,
        out_shape=out_shape,
        compiler_params=_cparams("parallel"),
    )(x, g.reshape(1, D), w)


def _mlstm_kernel(qk_ref, v_ref, o_ref, gt_ref, cw_ref, cb_ref, gb_ref, ng_ref, out_ref,
                  ext_ref, ct_ref, n_ref, m_ref):
    L = MLSTM_CHUNK
    dh = MLSTM_HEAD_DIM
    c = pl.program_id(1)

    @pl.when(c == 0)
    def _():
        ext_ref[0:8, :] = jnp.zeros((8, ext_ref.shape[1]), F32)
        ct_ref[...] = jnp.zeros(ct_ref.shape, F32)
        n_ref[...] = jnp.zeros(n_ref.shape, F32)
        m_ref[...] = jnp.zeros(m_ref.shape, F32)

    a = qk_ref[...]
    ext_ref[8:8 + L, :] = a
    conv = cb_ref[...]
    for j in range(CONV_WIDTH):
        conv = conv + cw_ref[j:j + 1, :] * ext_ref[pl.ds(8 - (CONV_WIDTH - 1) + j, L), :]
    ext_ref[0:8, :] = a[L - 8:L, :]
    qk = conv * _sigmoid(conv)

    G = gt_ref[...] + gb_ref[...]
    lane = lax.broadcasted_iota(I32, G.shape, 1)
    LG = jnp.where(lane < N_MLSTM_HEADS, G, _log_sigmoid(G))
    rr = lax.broadcasted_iota(I32, (L, L), 0)
    cc = lax.broadcasted_iota(I32, (L, L), 1)
    causal = cc <= rr
    tril = causal.astype(F32)
    bcum = jnp.dot(tril, LG, preferred_element_type=F32, precision=lax.Precision.HIGHEST)
    LGT = LG.T
    bcumT = bcum.T

    for h in range(N_MLSTM_HEADS):
        q = qk[:, h * dh:(h + 1) * dh]
        k = qk[:, MLSTM_WIDTH + h * dh:MLSTM_WIDTH + (h + 1) * dh] * (dh ** -0.5)
        v = v_ref[:, h * dh:(h + 1) * dh]
        i_col = LG[:, h:h + 1]
        i_row = LGT[h:h + 1, :]
        b_col = bcum[:, N_MLSTM_HEADS + h:N_MLSTM_HEADS + h + 1]
        b_row = bcumT[N_MLSTM_HEADS + h:N_MLSTM_HEADS + h + 1, :]
        m_prev = m_ref[h:h + 1, 0:1]
        ct = ct_ref[h]
        n_row = n_ref[h:h + 1, :]

        g_col = b_col + m_prev
        Dm = jnp.where(causal, b_col - b_row + i_row, NEG_INF)
        m_t = jnp.maximum(g_col, jnp.max(Dm, axis=1, keepdims=True))
        W = jnp.exp(Dm - m_t)
        inter = jnp.exp(g_col - m_t)
        s = _dot_nt(q, k) * W
        num = inter * _dot(q, ct) + _dot(s, v)
        den = inter * jnp.sum(q * n_row, axis=1, keepdims=True) + jnp.sum(s, axis=1, keepdims=True)
        hh = num / jnp.maximum(jnp.abs(den), jnp.exp(-m_t))

        F = b_col[L - 1:L, :]
        a_col = F - b_col + i_col
        m_new = jnp.maximum(F + m_prev, jnp.max(a_col, axis=0, keepdims=True))
        decay = jnp.exp(F + m_prev - m_new)
        w_col = jnp.exp(a_col - m_new)
        ct_ref[h] = decay * ct + _dot_tn(k, v * w_col)
        n_ref[h:h + 1, :] = decay * n_row + jnp.sum(k * w_col, axis=0, keepdims=True)
        m_ref[h:h + 1, :] = jnp.broadcast_to(m_new, (1, m_ref.shape[1]))

        hh = _sigmoid(o_ref[:, h * dh:(h + 1) * dh]) * hh
        hh = hh * lax.rsqrt(jnp.mean(hh * hh, axis=1, keepdims=True) + EPS)
        out_ref[:, h * dh:(h + 1) * dh] = hh * ng_ref[:, h * dh:(h + 1) * dh]


def _mlstm(qk_pre, v_m, o_m, gates, conv_w, conv_b, gate_b, norm_g, B, S):
    L = MLSTM_CHUNK
    nc = S // L
    W2 = 2 * MLSTM_WIDTH
    row = lambda b, c: (b * nc + c, 0)
    const = lambda b, c: (0, 0)
    return pl.pallas_call(
        _mlstm_kernel,
        grid=(B, nc),
        in_specs=[pl.BlockSpec((L, W2), row),
                  pl.BlockSpec((L, MLSTM_WIDTH), row),
                  pl.BlockSpec((L, MLSTM_WIDTH), row),
                  pl.BlockSpec((L, LANES), row),
                  pl.BlockSpec((CONV_WIDTH, W2), const),
                  pl.BlockSpec((1, W2), const),
                  pl.BlockSpec((1, LANES), const),
                  pl.BlockSpec((1, MLSTM_WIDTH), const)],
        out_specs=pl.BlockSpec((L, MLSTM_WIDTH), row),
        out_shape=jax.ShapeDtypeStruct((B * S, MLSTM_WIDTH), F32),
        scratch_shapes=[pltpu.VMEM((8 + L, W2), F32),
                        pltpu.VMEM((N_MLSTM_HEADS, MLSTM_HEAD_DIM, MLSTM_HEAD_DIM), F32),
                        pltpu.VMEM((8, MLSTM_HEAD_DIM), F32),
                        pltpu.VMEM((8, LANES), F32)],
        compiler_params=_cparams("parallel", "arbitrary"),
    )(qk_pre, v_m, o_m, gates, conv_w, conv_b.reshape(1, W2), gate_b.reshape(1, LANES),
      norm_g.reshape(1, MLSTM_WIDTH))


SB_Q_TILE = 512
SB_K_TILE = 128


def _sb_kernel(q_ref, k_ref, v_ref, g_ref, out_ref, q_ref_bf, acc_ref, rb_ref, *, tq, tk):
    i = pl.program_id(2)
    d = SB_HEAD_DIM
    nsub = tq // tk
    lane = lax.broadcasted_iota(I32, (1, LANES), 1)
    first = lane < d
    q_ref_bf[...] = (q_ref[...] * (d ** -0.5)).astype(BF16)
    acc_ref[...] = jnp.zeros(acc_ref.shape, F32)
    rb_ref[...] = jnp.zeros(rb_ref.shape, F32)
    rr = lax.broadcasted_iota(I32, (tk, 2 * tk), 0)
    cc = lax.broadcasted_iota(I32, (tk, 2 * tk), 1)
    tri_ones = jnp.where((rr >= cc) | (cc >= tk), 1.0, 0.0).astype(BF16)

    def tile(j, masked):
        start = pl.multiple_of(j * tk, tk)
        kb = k_ref[pl.ds(start, tk), :]
        vb = v_ref[pl.ds(start, tk), :]
        k2 = jnp.concatenate([jnp.where(first, kb, 0.0), jnp.where(first, 0.0, kb)], axis=0).astype(BF16)
        v2 = jnp.concatenate([jnp.where(first, vb, 0.0), jnp.where(first, 0.0, vb)], axis=0).astype(BF16)
        if masked:
            strict = (j * tk + lax.broadcasted_iota(I32, (tq, tk), 1)
                      < i * tq + lax.broadcasted_iota(I32, (tq, tk), 0))
        z2 = lax.dot_general(q_ref_bf[...], k2, (((1,), (1,)), ((), ())), preferred_element_type=F32)
        ps = []
        for h in range(2):
            z = z2[:, h * tk:(h + 1) * tk]
            lsm = -(jnp.maximum(z, 0.0) + jnp.log(1.0 + jnp.exp(-jnp.abs(z))))
            if masked:
                lsm = jnp.where(strict, lsm, 0.0)
            hi = lsm.astype(BF16)
            lo = (lsm - hi.astype(F32)).astype(BF16)
            cs2 = (jnp.dot(hi, tri_ones, preferred_element_type=F32)
                   + jnp.dot(lo, tri_ones, preferred_element_type=F32))
            rb = rb_ref[h]
            log_a = z + cs2[:, :tk] + rb
            if masked:
                log_a = jnp.where(strict, log_a, NEG_INF)
            ps.append(jnp.exp(log_a).astype(BF16))
            rb_ref[h] = rb + cs2[:, tk:]
        acc_ref[...] += jnp.dot(jnp.concatenate(ps, axis=1), v2, preferred_element_type=F32)

    def masked_step(jj, carry):
        tile(i * nsub + nsub - 1 - jj, True)
        return carry

    def plain_step(jj, carry):
        tile(i * nsub - 1 - jj, False)
        return carry

    lax.fori_loop(0, nsub, masked_step, 0)
    lax.fori_loop(0, i * nsub, plain_step, 0)
    acc = acc_ref[...]
    sq = acc * acc
    ms0 = jnp.sum(jnp.where(first, sq, 0.0), axis=1, keepdims=True) * (1.0 / d)
    ms1 = jnp.sum(jnp.where(first, 0.0, sq), axis=1, keepdims=True) * (1.0 / d)
    ms = jnp.where(first, ms0, ms1)
    out_ref[...] = acc * lax.rsqrt(ms + EPS) * g_ref[...]


def _sb_attention(qkv_s, norm_g, B, S):
    tq = min(SB_Q_TILE, S)
    tk = SB_K_TILE
    nq = S // tq
    npair = SB_WIDTH // LANES
    return pl.pallas_call(
        functools.partial(_sb_kernel, tq=tq, tk=tk),
        grid=(B, npair, nq),
        in_specs=[pl.BlockSpec((tq, LANES), lambda b, p, i: (b * nq + i, p)),
                  pl.BlockSpec((S, LANES), lambda b, p, i: (b, npair + p)),
                  pl.BlockSpec((S, LANES), lambda b, p, i: (b, 2 * npair + p)),
                  pl.BlockSpec((1, LANES), lambda b, p, i: (0, p))],
        out_specs=pl.BlockSpec((tq, LANES), lambda b, p, i: (b * nq + i, p)),
        out_shape=jax.ShapeDtypeStruct((B * S, SB_WIDTH), F32),
        scratch_shapes=[pltpu.VMEM((tq, LANES), BF16),
                        pltpu.VMEM((tq, LANES), F32),
                        pltpu.VMEM((2, tq, LANES), F32)],
        compiler_params=_cparams("parallel", "parallel", "arbitrary"),
    )(qkv_s, qkv_s, qkv_s, norm_g.reshape(1, SB_WIDTH))


def _xattn_kernel(x_ref, hm_ref, hs_ref, wo1_ref, wo2_ref, g_ref, wq_ref, kv_ref, wo_ref, out_ref):
    D = x_ref.shape[1]
    dh = D // N_XATTN_HEADS
    x1 = (x_ref[...]
          + jnp.dot(hm_ref[...].astype(BF16), wo1_ref[...], preferred_element_type=F32)
          + jnp.dot(hs_ref[...].astype(BF16), wo2_ref[...], preferred_element_type=F32))
    xn = _rms(x1, g_ref[...])
    q = jnp.dot(xn.astype(BF16), wq_ref[...], preferred_element_type=F32)
    heads = []
    for h in range(N_XATTN_HEADS):
        qh = q[:, h * dh:(h + 1) * dh]
        kh = kv_ref[:, h * dh:(h + 1) * dh]
        vh = kv_ref[:, D + h * dh:D + (h + 1) * dh]
        s = _dot_nt(qh, kh) * (dh ** -0.5)
        s = s - jnp.max(s, axis=1, keepdims=True)
        p = jnp.exp(s)
        p = p / jnp.sum(p, axis=1, keepdims=True)
        heads.append(_dot(p, vh))
    o = jnp.concatenate(heads, axis=1)
    out_ref[...] = x1 + jnp.dot(o.astype(BF16), wo_ref[...], preferred_element_type=F32)


def _mixer_out_xattn(x, hm, hs, wo1, wo2, g, wq, kv, wo, B, S, M, tm=256):
    T, D = x.shape
    nt = S // tm
    row = lambda i: (i, 0)
    const = lambda i: (0, 0)
    return pl.pallas_call(
        _xattn_kernel,
        grid=(T // tm,),
        in_specs=[pl.BlockSpec((tm, D), row),
                  pl.BlockSpec((tm, MLSTM_WIDTH), row),
                  pl.BlockSpec((tm, SB_WIDTH), row),
                  pl.BlockSpec((MLSTM_WIDTH, D), const),
                  pl.BlockSpec((SB_WIDTH, D), const),
                  pl.BlockSpec((1, D), const),
                  pl.BlockSpec((D, D), const),
                  pl.BlockSpec((M, 2 * D), lambda i: (i // nt, 0)),
                  pl.BlockSpec((D, D), const)],
        out_specs=pl.BlockSpec((tm, D), row),
        out_shape=jax.ShapeDtypeStruct((T, D), F32),
        compiler_params=_cparams("parallel"),
    )(x, hm, hs, wo1, wo2, g.reshape(1, D), wq, kv, wo)


def _top_rows(s, ids, payload, k):
    big = jnp.int32(2 ** 30)
    vals, pays = [], []
    for _ in range(k):
        m = jnp.max(s, axis=0, keepdims=True)
        sel = jnp.min(jnp.where(s == m, ids, big), axis=0, keepdims=True)
        hit = ids == sel
        vals.append(m)
        pays.append(jnp.sum(jnp.where(hit, payload, 0), axis=0, keepdims=True))
        s = jnp.where(hit, NEG_INF, s)
    return jnp.concatenate(vals, axis=0), jnp.concatenate(pays, axis=0)


def _peer_route_kernel(q_ref, sk_ref, exp_ref, gate_ref):
    tm = q_ref.shape[0]
    K = PEER_TOPK
    key_ids = lax.broadcasted_iota(I32, (PEER_N_KEYS, tm), 0)
    exp_rows, gate_rows = [], []
    for h in range(PEER_HEADS):
        tops = []
        for p in range(2):
            hp = 2 * h + p
            qc = q_ref[:, hp * PEER_HALF:(hp + 1) * PEER_HALF]
            s = _dot_nt(sk_ref[hp], qc)
            tops.append(_top_rows(s, key_ids, key_ids, K))
        (v0, i0), (v1, i1) = tops
        cand, cpos, cexp = [], [], []
        for a in range(K):
            nb = K // (a + 1)
            cand.append(v0[a:a + 1, :] + v1[0:nb, :])
            cpos.append(a * K + lax.broadcasted_iota(I32, (nb, tm), 0))
            cexp.append(i0[a:a + 1, :] * PEER_N_KEYS + i1[0:nb, :])
        best, experts = _top_rows(jnp.concatenate(cand, axis=0), jnp.concatenate(cpos, axis=0),
                                  jnp.concatenate(cexp, axis=0), K)
        e = jnp.exp(best - best[0:1, :])
        gate_rows.append(e / jnp.sum(e, axis=0, keepdims=True))
        exp_rows.append(experts)
    exp_ref[0] = jnp.concatenate(exp_rows, axis=0)
    gate_ref[0] = jnp.concatenate(gate_rows, axis=0)


def _peer_route(q, subkeys, tm=128):
    T = q.shape[0]
    nb = T // tm
    return pl.pallas_call(
        _peer_route_kernel,
        grid=(nb,),
        in_specs=[pl.BlockSpec((tm, q.shape[1]), lambda i: (i, 0)),
                  pl.BlockSpec(subkeys.shape, lambda i: (0, 0, 0))],
        out_specs=[pl.BlockSpec((1, PEER_E, tm), lambda i: (i, 0, 0)),
                   pl.BlockSpec((1, PEER_E, tm), lambda i: (i, 0, 0))],
        out_shape=[jax.ShapeDtypeStruct((nb, PEER_E, tm), I32),
                   jax.ShapeDtypeStruct((nb, PEER_E, tm), F32)],
        compiler_params=_cparams("parallel"),
    )(q, subkeys)


SC_LANES = 16
SC_EXPERT_BLOCK = 16
SC_RING = 4


def _sc_params():
    return dataclasses.replace(pltpu.CompilerParams(), needs_layout_passes=False)


def _unpack_words(w):
    lo = lax.bitcast_convert_type(lax.shift_left(w, jnp.int32(16)), F32)
    hi = lax.bitcast_convert_type(jnp.bitwise_and(w, jnp.int32(-65536)), F32)
    return lo, hi


def _sc_expert_walk(ta_hbm, tb_hbm, idx_v, bufs, sems, nt, per_block, per_token_start=None, per_token_end=None):
    EB = SC_EXPERT_BLOCK
    neb = PEER_E // EB
    nitems = nt * neb
    ring = SC_RING
    assert neb % ring == 0

    def copies(item, slot):
        iv = idx_v[item // neb, pl.ds((item % neb) * EB, EB)]
        return (pltpu.make_async_copy(ta_hbm.at[iv], bufs[2 * slot], sems[2 * slot]),
                pltpu.make_async_copy(tb_hbm.at[iv], bufs[2 * slot + 1], sems[2 * slot + 1]))

    for s in range(ring - 1):
        for cp in copies(s, s):
            cp.start()

    @pl.loop(0, nitems, step=ring)
    def _(it0):
        for s in range(ring):
            item = it0 + s
            nxt = item + ring - 1

            @pl.when(nxt < nitems)
            def _():
                for cp in copies(nxt, (s + ring - 1) % ring):
                    cp.start()

            for cp in copies(item, s):
                cp.wait()
            t = item // neb
            if s == 0 and per_token_start is not None:
                @pl.when(it0 % neb == 0)
                def _():
                    per_token_start(t)
            per_block(t, item % neb, bufs[2 * s], bufs[2 * s + 1])
            if s == ring - 1 and per_token_end is not None:
                @pl.when((it0 + ring) % neb == 0)
                def _():
                    per_token_end(t)


def _sc_ring_scratch(row_words):
    return ([pltpu.VMEM((SC_EXPERT_BLOCK, row_words), I32)] * (2 * SC_RING)
            + [pltpu.SemaphoreType.DMA] * (2 * SC_RING + 2))


def _peer_udot(ta, tb, idx, xn):
    T, E = idx.shape
    D = xn.shape[1]
    Q = D // 4
    EB = SC_EXPERT_BLOCK
    L = SC_LANES
    mesh = plsc.VectorSubcoreMesh(core_axis_name="core", subcore_axis_name="subcore")
    n_cores = mesh.num_cores
    nw = n_cores * mesh.num_subcores
    assert T % nw == 0 and E == PEER_E and ta.shape[1] == Q
    nt = T // nw

    @functools.partial(
        pl.kernel, out_type=jax.ShapeDtypeStruct((T, E), F32), mesh=mesh, compiler_params=_sc_params(),
        scratch_types=[pltpu.VMEM((nt, E), I32), pltpu.VMEM((2, D), F32), pltpu.VMEM((nt, E), F32)]
                      + _sc_ring_scratch(Q))
    def udot_kernel(ta_hbm, tb_hbm, idx_hbm, x_hbm, h_hbm, idx_v, x_v, h_v, *rest):
        bufs, sems, xsems = rest[:2 * SC_RING], rest[2 * SC_RING:4 * SC_RING], rest[4 * SC_RING:]
        t0 = (lax.axis_index("subcore") * n_cores + lax.axis_index("core")) * nt
        pltpu.sync_copy(idx_hbm.at[pl.ds(t0, nt)], idx_v)
        lane = lax.iota(I32, L)

        def xcopy(t, par):
            return pltpu.make_async_copy(x_hbm.at[t0 + t], x_v.at[par], xsems[par])

        xcopy(0, 0).start()

        def token_start(t):
            for par in range(2):
                @pl.when(t % 2 == par)
                def _():
                    xcopy(t, par).wait()

                    @pl.when(t + 1 < nt)
                    def _():
                        xcopy(t + 1, 1 - par).start()

        def block(t, eb, rows_a, rows_b):
            xs = t % 2
            accs = tuple(jnp.zeros((L,), F32) for _ in range(EB))
            for half, rows in enumerate((rows_a, rows_b)):
                def cbody(c, accs, rows=rows, half=half):
                    xl = x_v[xs, pl.ds(half * 2 * Q + c * L, L)]
                    xh = x_v[xs, pl.ds(half * 2 * Q + Q + c * L, L)]
                    out = []
                    for e in range(EB):
                        lo, hi = _unpack_words(rows[e, pl.ds(c * L, L)])
                        out.append(accs[e] + lo * xl + hi * xh)
                    return tuple(out)
                accs = lax.fori_loop(0, Q // L, cbody, accs)
            res = jnp.zeros((L,), F32)
            for e in range(EB):
                res = jnp.where(lane == e, jnp.sum(accs[e]), res)
            h_v[t, pl.ds(eb * EB, EB)] = res

        _sc_expert_walk(ta_hbm, tb_hbm, idx_v, bufs, sems, nt, block, per_token_start=token_start)
        pltpu.sync_copy(h_v, h_hbm.at[pl.ds(t0, nt)])

    return udot_kernel(ta, tb, idx, xn)


def _peer_vsum(ta, tb, idx, act):
    T, E = idx.shape
    Q = ta.shape[1]
    D = 4 * Q
    EB = SC_EXPERT_BLOCK
    L = SC_LANES
    mesh = plsc.VectorSubcoreMesh(core_axis_name="core", subcore_axis_name="subcore")
    n_cores = mesh.num_cores
    nw = n_cores * mesh.num_subcores
    assert T % nw == 0 and E == PEER_E
    nt = T // nw
    assert nt >= 2

    @functools.partial(
        pl.kernel, out_type=jax.ShapeDtypeStruct((T, D), F32), mesh=mesh, compiler_params=_sc_params(),
        scratch_types=[pltpu.VMEM((nt, E), I32), pltpu.VMEM((nt, E), F32), pltpu.VMEM((2, D), F32)]
                      + _sc_ring_scratch(Q))
    def vsum_kernel(ta_hbm, tb_hbm, idx_hbm, act_hbm, y_hbm, idx_v, act_v, y_v, *rest):
        bufs, sems, ysems = rest[:2 * SC_RING], rest[2 * SC_RING:4 * SC_RING], rest[4 * SC_RING:]
        t0 = (lax.axis_index("subcore") * n_cores + lax.axis_index("core")) * nt
        pltpu.sync_copy(idx_hbm.at[pl.ds(t0, nt)], idx_v)
        pltpu.sync_copy(act_hbm.at[pl.ds(t0, nt)], act_v)
        lane = lax.iota(I32, L)
        zero = jnp.zeros((L,), F32)

        def ycopy(t, par):
            return pltpu.make_async_copy(y_v.at[par], y_hbm.at[t0 + t], ysems[par])

        def token_start(t):
            for par in range(2):
                @pl.when((t % 2 == par) & (t >= 2))
                def _():
                    ycopy(t - 2, par).wait()

            @pl.loop(0, D // L)
            def _(c):
                y_v[t % 2, pl.ds(c * L, L)] = zero

        def token_end(t):
            for par in range(2):
                @pl.when(t % 2 == par)
                def _():
                    ycopy(t, par).start()

        def block(t, eb, rows_a, rows_b):
            ys = t % 2
            av = act_v[t, pl.ds(eb * EB, EB)]
            ab = [jnp.full((L,), jnp.sum(jnp.where(lane == e, av, zero)), F32) for e in range(EB)]
            for half, rows in enumerate((rows_a, rows_b)):
                @pl.loop(0, Q // L)
                def _(c, rows=rows, half=half):
                    alo = zero
                    ahi = zero
                    for e in range(EB):
                        lo, hi = _unpack_words(rows[e, pl.ds(c * L, L)])
                        alo = alo + ab[e] * lo
                        ahi = ahi + ab[e] * hi
                    o = half * 2 * Q + c * L
                    y_v[ys, pl.ds(o, L)] = y_v[ys, pl.ds(o, L)] + alo
                    y_v[ys, pl.ds(o + Q, L)] = y_v[ys, pl.ds(o + Q, L)] + ahi

        _sc_expert_walk(ta_hbm, tb_hbm, idx_v, bufs, sems, nt, block,
                        per_token_start=token_start, per_token_end=token_end)
        for tl in (nt - 2, nt - 1):
            ycopy(tl, tl % 2).wait()

    return vsum_kernel(ta, tb, idx, act)


def _peer_act_kernel(h_ref, g_ref, a_ref):
    h = h_ref[...]
    a_ref[...] = 0.5 * h * (1.0 + lax.erf(h * (2.0 ** -0.5))) * g_ref[...]


def _peer_act(h, gates, tm=512):
    T, E = h.shape
    tm = min(tm, T)
    spec = pl.BlockSpec((tm, E), lambda i: (i, 0))
    return pl.pallas_call(_peer_act_kernel, grid=(T // tm,), in_specs=[spec, spec], out_specs=spec,
                          out_shape=jax.ShapeDtypeStruct((T, E), F32),
                          compiler_params=_cparams("parallel"))(h, gates)


def _residual_norm_kernel(x_ref, y_ref, g_ref, o_ref):
    o_ref[...] = _rms(x_ref[...] + y_ref[...], g_ref[...])


def _residual_norm(x, y, g, tm=512):
    T, D = x.shape
    tm = min(tm, T)
    spec = pl.BlockSpec((tm, D), lambda i: (i, 0))
    return pl.pallas_call(_residual_norm_kernel, grid=(T // tm,),
                          in_specs=[spec, spec, pl.BlockSpec((1, D), lambda i: (0, 0))], out_specs=spec,
                          out_shape=jax.ShapeDtypeStruct((T, D), F32),
                          compiler_params=_cparams("parallel"))(x, y, g.reshape(1, D))


def _pack_bf16_pairs(w):
    Q = w.shape[1] // 4
    bits = lax.bitcast_convert_type(w.astype(BF16), jnp.uint16).astype(jnp.uint32)
    tables = []
    for s in range(2):
        words = bits[:, 2 * s * Q:(2 * s + 1) * Q] | (bits[:, (2 * s + 1) * Q:(2 * s + 2) * Q] << 16)
        tables.append(lax.bitcast_convert_type(words, I32))
    return tables


PEER_TOKEN_CHUNK = 4096
BATCH_GROUPS = 4


def kernel(x, mem, mix_norm_g, w_in, conv_w, conv_b, igate_b, fgate_b, mlstm_norm_g, sb_norm_g, w_out, xattn_norm_g, mem_norm_g, xattn_wq, xattn_wkv, xattn_wo, ffn_norm_g, peer_wq, peer_subkeys, peer_u, peer_v, final_norm_g):
    B, S, D = x.shape
    M = mem.shape[1]
    assert w_in.shape[0] == 1, "the output norm is applied right after the PEER sum; only one layer is supported"
    MW = MLSTM_WIDTH
    E = PEER_E
    n_groups = BATCH_GROUPS if B % BATCH_GROUPS == 0 else 1
    Bg = B // n_groups
    Tg = Bg * S

    w = w_in[0]
    gate_cols = jnp.pad(w[:, 4 * MW:4 * MW + 2 * N_MLSTM_HEADS], ((0, 0), (0, LANES - 2 * N_MLSTM_HEADS)))
    w_all = jnp.concatenate([w[:, :4 * MW], gate_cols, w[:, 4 * MW + 2 * N_MLSTM_HEADS:]], axis=1).astype(BF16)
    gate_b = jnp.pad(jnp.concatenate([igate_b[0], fgate_b[0]]), (0, LANES - 2 * N_MLSTM_HEADS))
    wo = w_out[0].astype(BF16)
    wkv = xattn_wkv[0].astype(BF16)
    wq = xattn_wq[0].astype(BF16)
    wxo = xattn_wo[0].astype(BF16)
    wpq = peer_wq[0].astype(BF16)
    subkeys = peer_subkeys[0].reshape(PEER_HEADS * 2, PEER_N_KEYS, PEER_HALF).astype(BF16)
    ua, ub = _pack_bf16_pairs(peer_u[0])
    va, vb = _pack_bf16_pairs(peer_v[0])
    Tc = min(PEER_TOKEN_CHUNK, Tg)

    outs = []
    for g in range(n_groups):
        xt = x[g * Bg:(g + 1) * Bg].reshape(Tg, D)
        qk_pre, v_m, o_m, gates, qkv_s = _norm_proj(
            xt, mix_norm_g[0], w_all, (2 * MW, MW, MW, LANES, 3 * SB_WIDTH))
        h_m = _mlstm(qk_pre, v_m, o_m, gates, conv_w[0], conv_b[0], gate_b, mlstm_norm_g[0], Bg, S)
        h_s = _sb_attention(qkv_s, sb_norm_g[0], Bg, S)
        (kv,) = _norm_proj(mem[g * Bg:(g + 1) * Bg].reshape(Bg * M, D), mem_norm_g[0], wkv, (2 * D,))
        x2 = _mixer_out_xattn(xt, h_m, h_s, wo[:MW], wo[MW:], xattn_norm_g[0], wq, kv, wxo, Bg, S, M)
        xn3, q_peer = _norm_proj(x2, ffn_norm_g[0], wpq, (PEER_HEADS * 2 * PEER_HALF,), emit_norm=True)
        experts_t, gates_t = _peer_route(q_peer, subkeys)
        experts = experts_t.transpose(0, 2, 1).reshape(Tg, E)
        gate_rows = gates_t.transpose(0, 2, 1).reshape(Tg, E)
        for c in range(Tg // Tc):
            rows = slice(c * Tc, (c + 1) * Tc)
            idx_c = experts[rows]
            h = _peer_udot(ua, ub, idx_c, xn3[rows])
            act = _peer_act(h, gate_rows[rows])
            y = _peer_vsum(va, vb, idx_c, act)
            outs.append(_residual_norm(x2[rows], y, final_norm_g))
    out = jnp.concatenate(outs, axis=0) if len(outs) > 1 else outs[0]
    return out.reshape(B, S, D)
```

```python
import dataclasses
import functools

import jax
import jax.numpy as jnp
from jax import lax
from jax.experimental import pallas as pl
from jax.experimental.pallas import tpu as pltpu
from jax.experimental.pallas import tpu_sc as plsc

F32 = jnp.float32
BF16 = jnp.bfloat16
I32 = jnp.int32

EPS = 1e-6
LANES = 128
N_MLSTM_HEADS = 4
MLSTM_HEAD_DIM = 128
MLSTM_WIDTH = N_MLSTM_HEADS * MLSTM_HEAD_DIM
N_SB_HEADS = 8
SB_HEAD_DIM = 64
SB_WIDTH = N_SB_HEADS * SB_HEAD_DIM
CONV_WIDTH = 4
MLSTM_CHUNK = 128
N_XATTN_HEADS = 4
PEER_HEADS = 8
PEER_N_KEYS = 128
PEER_TOPK = 16
PEER_HALF = 128
PEER_E = PEER_HEADS * PEER_TOPK
VMEM_LIMIT = 48 * 1024 * 1024
NEG_INF = float("-inf")


def _cparams(*sem):
    return pltpu.CompilerParams(dimension_semantics=sem, vmem_limit_bytes=VMEM_LIMIT)


def _rms(x, g):
    return x * lax.rsqrt(jnp.mean(x * x, axis=-1, keepdims=True) + EPS) * g


def _sigmoid(x):
    return 1.0 / (1.0 + jnp.exp(-x))


def _log_sigmoid(x):
    return jnp.minimum(x, 0.0) - jnp.log(1.0 + jnp.exp(-jnp.abs(x)))


def _dot(a, b):
    return jnp.dot(a.astype(BF16), b.astype(BF16), preferred_element_type=F32)


def _dot_nt(a, b):
    return lax.dot_general(a.astype(BF16), b.astype(BF16), (((1,), (1,)), ((), ())),
                           preferred_element_type=F32)


def _dot_tn(a, b):
    return lax.dot_general(a.astype(BF16), b.astype(BF16), (((0,), (0,)), ((), ())),
                           preferred_element_type=F32)


def _norm_proj_kernel(x_ref, g_ref, w_ref, *out_refs, splits, emit_norm, n_after):
    out_refs = out_refs[n_after:]
    xn = _rms(x_ref[...], g_ref[...])
    y = jnp.dot(xn.astype(BF16), w_ref[...], preferred_element_type=F32)
    outs = out_refs
    if emit_norm:
        outs[0][...] = xn
        outs = outs[1:]
    off = 0
    for o_ref, width in zip(outs, splits):
        o_ref[...] = y[:, off:off + width]
        off += width


def _norm_proj(x, g, w, splits, emit_norm=False, tm=256, after=()):
    T, D = x.shape
    N = w.shape[1]
    assert sum(splits) == N and T % tm == 0
    out_shape = [jax.ShapeDtypeStruct((T, wd), F32) for wd in splits]
    out_specs = [pl.BlockSpec((tm, wd), lambda i: (i, 0)) for wd in splits]
    if emit_norm:
        out_shape = [jax.ShapeDtypeStruct((T, D), F32)] + out_shape
        out_specs = [pl.BlockSpec((tm, D), lambda i: (i, 0))] + out_specs
    return pl.pallas_call(
        functools.partial(_norm_proj_kernel, splits=tuple(splits), emit_norm=emit_norm, n_after=len(after)),
        grid=(T // tm,),
        in_specs=[pl.BlockSpec((tm, D), lambda i: (i, 0)),
                  pl.BlockSpec((1, D), lambda i: (0, 0)),
                  pl.BlockSpec((D, N), lambda i: (0, 0))]
                 + [pl.BlockSpec((8, LANES), lambda i: (0, 0)) for _ in after],
        out_specs=out_specs,
        out_shape=out_shape,
        compiler_params=_cparams("parallel"),
    )(x, g.reshape(1, D), w, *after)


def _mlstm_kernel(qk_ref, v_ref, o_ref, gt_ref, cw_ref, cb_ref, gb_ref, ng_ref, out_ref,
                  ext_ref, ct_ref, n_ref, m_ref):
    L = MLSTM_CHUNK
    dh = MLSTM_HEAD_DIM
    c = pl.program_id(1)

    @pl.when(c == 0)
    def _():
        ext_ref[0:8, :] = jnp.zeros((8, ext_ref.shape[1]), F32)
        ct_ref[...] = jnp.zeros(ct_ref.shape, F32)
        n_ref[...] = jnp.zeros(n_ref.shape, F32)
        m_ref[...] = jnp.zeros(m_ref.shape, F32)

    a = qk_ref[...]
    ext_ref[8:8 + L, :] = a
    conv = cb_ref[...]
    for j in range(CONV_WIDTH):
        conv = conv + cw_ref[j:j + 1, :] * ext_ref[pl.ds(8 - (CONV_WIDTH - 1) + j, L), :]
    ext_ref[0:8, :] = a[L - 8:L, :]
    qk = conv * _sigmoid(conv)

    G = gt_ref[...] + gb_ref[...]
    lane = lax.broadcasted_iota(I32, G.shape, 1)
    LG = jnp.where(lane < N_MLSTM_HEADS, G, _log_sigmoid(G))
    rr = lax.broadcasted_iota(I32, (L, L), 0)
    cc = lax.broadcasted_iota(I32, (L, L), 1)
    causal = cc <= rr
    tril = causal.astype(F32)
    bcum = jnp.dot(tril, LG, preferred_element_type=F32, precision=lax.Precision.HIGHEST)
    LGT = LG.T
    bcumT = bcum.T

    for h in range(N_MLSTM_HEADS):
        q = qk[:, h * dh:(h + 1) * dh]
        k = qk[:, MLSTM_WIDTH + h * dh:MLSTM_WIDTH + (h + 1) * dh] * (dh ** -0.5)
        v = v_ref[:, h * dh:(h + 1) * dh]
        i_col = LG[:, h:h + 1]
        i_row = LGT[h:h + 1, :]
        b_col = bcum[:, N_MLSTM_HEADS + h:N_MLSTM_HEADS + h + 1]
        b_row = bcumT[N_MLSTM_HEADS + h:N_MLSTM_HEADS + h + 1, :]
        m_prev = m_ref[h:h + 1, 0:1]
        ct = ct_ref[h]
        n_row = n_ref[h:h + 1, :]

        g_col = b_col + m_prev
        Dm = jnp.where(causal, b_col - b_row + i_row, NEG_INF)
        m_t = jnp.maximum(g_col, jnp.max(Dm, axis=1, keepdims=True))
        W = jnp.exp(Dm - m_t)
        inter = jnp.exp(g_col - m_t)
        s = _dot_nt(q, k) * W
        num = inter * _dot(q, ct) + _dot(s, v)
        den = inter * jnp.sum(q * n_row, axis=1, keepdims=True) + jnp.sum(s, axis=1, keepdims=True)
        hh = num / jnp.maximum(jnp.abs(den), jnp.exp(-m_t))

        F = b_col[L - 1:L, :]
        a_col = F - b_col + i_col
        m_new = jnp.maximum(F + m_prev, jnp.max(a_col, axis=0, keepdims=True))
        decay = jnp.exp(F + m_prev - m_new)
        w_col = jnp.exp(a_col - m_new)
        ct_ref[h] = decay * ct + _dot_tn(k, v * w_col)
        n_ref[h:h + 1, :] = decay * n_row + jnp.sum(k * w_col, axis=0, keepdims=True)
        m_ref[h:h + 1, :] = jnp.broadcast_to(m_new, (1, m_ref.shape[1]))

        hh = _sigmoid(o_ref[:, h * dh:(h + 1) * dh]) * hh
        hh = hh * lax.rsqrt(jnp.mean(hh * hh, axis=1, keepdims=True) + EPS)
        out_ref[:, h * dh:(h + 1) * dh] = hh * ng_ref[:, h * dh:(h + 1) * dh]


def _mlstm(qk_pre, v_m, o_m, gates, conv_w, conv_b, gate_b, norm_g, B, S):
    L = MLSTM_CHUNK
    nc = S // L
    W2 = 2 * MLSTM_WIDTH
    row = lambda b, c: (b * nc + c, 0)
    const = lambda b, c: (0, 0)
    return pl.pallas_call(
        _mlstm_kernel,
        grid=(B, nc),
        in_specs=[pl.BlockSpec((L, W2), row),
                  pl.BlockSpec((L, MLSTM_WIDTH), row),
                  pl.BlockSpec((L, MLSTM_WIDTH), row),
                  pl.BlockSpec((L, LANES), row),
                  pl.BlockSpec((CONV_WIDTH, W2), const),
                  pl.BlockSpec((1, W2), const),
                  pl.BlockSpec((1, LANES), const),
                  pl.BlockSpec((1, MLSTM_WIDTH), const)],
        out_specs=pl.BlockSpec((L, MLSTM_WIDTH), row),
        out_shape=jax.ShapeDtypeStruct((B * S, MLSTM_WIDTH), F32),
        scratch_shapes=[pltpu.VMEM((8 + L, W2), F32),
                        pltpu.VMEM((N_MLSTM_HEADS, MLSTM_HEAD_DIM, MLSTM_HEAD_DIM), F32),
                        pltpu.VMEM((8, MLSTM_HEAD_DIM), F32),
                        pltpu.VMEM((8, LANES), F32)],
        compiler_params=_cparams("parallel", "arbitrary"),
    )(qk_pre, v_m, o_m, gates, conv_w, conv_b.reshape(1, W2), gate_b.reshape(1, LANES),
      norm_g.reshape(1, MLSTM_WIDTH))


SB_Q_TILE = 512
SB_K_TILE = 128


def _sb_kernel(q_ref, k_ref, v_ref, g_ref, out_ref, q_ref_bf, acc_ref, rb_ref, *, tq, tk):
    i = pl.program_id(2)
    d = SB_HEAD_DIM
    nsub = tq // tk
    lane = lax.broadcasted_iota(I32, (1, LANES), 1)
    first = lane < d
    q_ref_bf[...] = (q_ref[...] * (d ** -0.5)).astype(BF16)
    acc_ref[...] = jnp.zeros(acc_ref.shape, F32)
    rb_ref[...] = jnp.zeros(rb_ref.shape, F32)
    rr = lax.broadcasted_iota(I32, (tk, 2 * tk), 0)
    cc = lax.broadcasted_iota(I32, (tk, 2 * tk), 1)
    tri_ones = jnp.where((rr >= cc) | (cc >= tk), 1.0, 0.0).astype(BF16)

    def tile(j, masked):
        start = pl.multiple_of(j * tk, tk)
        kb = k_ref[pl.ds(start, tk), :]
        vb = v_ref[pl.ds(start, tk), :]
        k2 = jnp.concatenate([jnp.where(first, kb, 0.0), jnp.where(first, 0.0, kb)], axis=0).astype(BF16)
        v2 = jnp.concatenate([jnp.where(first, vb, 0.0), jnp.where(first, 0.0, vb)], axis=0).astype(BF16)
        if masked:
            strict = (j * tk + lax.broadcasted_iota(I32, (tq, tk), 1)
                      < i * tq + lax.broadcasted_iota(I32, (tq, tk), 0))
        z2 = lax.dot_general(q_ref_bf[...], k2, (((1,), (1,)), ((), ())), preferred_element_type=F32)
        ps = []
        for h in range(2):
            z = z2[:, h * tk:(h + 1) * tk]
            lsm = -(jnp.maximum(z, 0.0) + jnp.log(1.0 + jnp.exp(-jnp.abs(z))))
            if masked:
                lsm = jnp.where(strict, lsm, 0.0)
            hi = lsm.astype(BF16)
            lo = (lsm - hi.astype(F32)).astype(BF16)
            cs2 = (jnp.dot(hi, tri_ones, preferred_element_type=F32)
                   + jnp.dot(lo, tri_ones, preferred_element_type=F32))
            rb = rb_ref[h]
            log_a = z + cs2[:, :tk] + rb
            if masked:
                log_a = jnp.where(strict, log_a, NEG_INF)
            ps.append(jnp.exp(log_a).astype(BF16))
            rb_ref[h] = rb + cs2[:, tk:]
        acc_ref[...] += jnp.dot(jnp.concatenate(ps, axis=1), v2, preferred_element_type=F32)

    def masked_step(jj, carry):
        tile(i * nsub + nsub - 1 - jj, True)
        return carry

    def plain_step(jj, carry):
        tile(i * nsub - 1 - jj, False)
        return carry

    lax.fori_loop(0, nsub, masked_step, 0)
    lax.fori_loop(0, i * nsub, plain_step, 0)
    acc = acc_ref[...]
    sq = acc * acc
    ms0 = jnp.sum(jnp.where(first, sq, 0.0), axis=1, keepdims=True) * (1.0 / d)
    ms1 = jnp.sum(jnp.where(first, 0.0, sq), axis=1, keepdims=True) * (1.0 / d)
    ms = jnp.where(first, ms0, ms1)
    out_ref[...] = acc * lax.rsqrt(ms + EPS) * g_ref[...]


def _sb_attention(qkv_s, norm_g, B, S):
    tq = min(SB_Q_TILE, S)
    tk = SB_K_TILE
    nq = S // tq
    npair = SB_WIDTH // LANES
    return pl.pallas_call(
        functools.partial(_sb_kernel, tq=tq, tk=tk),
        grid=(B, npair, nq),
        in_specs=[pl.BlockSpec((tq, LANES), lambda b, p, i: (b * nq + i, p)),
                  pl.BlockSpec((S, LANES), lambda b, p, i: (b, npair + p)),
                  pl.BlockSpec((S, LANES), lambda b, p, i: (b, 2 * npair + p)),
                  pl.BlockSpec((1, LANES), lambda b, p, i: (0, p))],
        out_specs=pl.BlockSpec((tq, LANES), lambda b, p, i: (b * nq + i, p)),
        out_shape=jax.ShapeDtypeStruct((B * S, SB_WIDTH), F32),
        scratch_shapes=[pltpu.VMEM((tq, LANES), BF16),
                        pltpu.VMEM((tq, LANES), F32),
                        pltpu.VMEM((2, tq, LANES), F32)],
        compiler_params=_cparams("parallel", "parallel", "arbitrary"),
    )(qkv_s, qkv_s, qkv_s, norm_g.reshape(1, SB_WIDTH))


def _xattn_kernel(x_ref, hm_ref, hs_ref, wo1_ref, wo2_ref, g_ref, wq_ref, kv_ref, wo_ref, out_ref):
    D = x_ref.shape[1]
    dh = D // N_XATTN_HEADS
    x1 = (x_ref[...]
          + jnp.dot(hm_ref[...].astype(BF16), wo1_ref[...], preferred_element_type=F32)
          + jnp.dot(hs_ref[...].astype(BF16), wo2_ref[...], preferred_element_type=F32))
    xn = _rms(x1, g_ref[...])
    q = jnp.dot(xn.astype(BF16), wq_ref[...], preferred_element_type=F32)
    heads = []
    for h in range(N_XATTN_HEADS):
        qh = q[:, h * dh:(h + 1) * dh]
        kh = kv_ref[:, h * dh:(h + 1) * dh]
        vh = kv_ref[:, D + h * dh:D + (h + 1) * dh]
        s = _dot_nt(qh, kh) * (dh ** -0.5)
        s = s - jnp.max(s, axis=1, keepdims=True)
        p = jnp.exp(s)
        p = p / jnp.sum(p, axis=1, keepdims=True)
        heads.append(_dot(p, vh))
    o = jnp.concatenate(heads, axis=1)
    out_ref[...] = x1 + jnp.dot(o.astype(BF16), wo_ref[...], preferred_element_type=F32)


def _mixer_out_xattn(x, hm, hs, wo1, wo2, g, wq, kv, wo, B, S, M, tm=256):
    T, D = x.shape
    nt = S // tm
    row = lambda i: (i, 0)
    const = lambda i: (0, 0)
    return pl.pallas_call(
        _xattn_kernel,
        grid=(T // tm,),
        in_specs=[pl.BlockSpec((tm, D), row),
                  pl.BlockSpec((tm, MLSTM_WIDTH), row),
                  pl.BlockSpec((tm, SB_WIDTH), row),
                  pl.BlockSpec((MLSTM_WIDTH, D), const),
                  pl.BlockSpec((SB_WIDTH, D), const),
                  pl.BlockSpec((1, D), const),
                  pl.BlockSpec((D, D), const),
                  pl.BlockSpec((M, 2 * D), lambda i: (i // nt, 0)),
                  pl.BlockSpec((D, D), const)],
        out_specs=pl.BlockSpec((tm, D), row),
        out_shape=jax.ShapeDtypeStruct((T, D), F32),
        compiler_params=_cparams("parallel"),
    )(x, hm, hs, wo1, wo2, g.reshape(1, D), wq, kv, wo)


def _top_rows(s, ids, payload, k):
    big = jnp.int32(2 ** 30)
    vals, pays = [], []
    for _ in range(k):
        m = jnp.max(s, axis=0, keepdims=True)
        sel = jnp.min(jnp.where(s == m, ids, big), axis=0, keepdims=True)
        hit = ids == sel
        vals.append(m)
        pays.append(jnp.sum(jnp.where(hit, payload, 0), axis=0, keepdims=True))
        s = jnp.where(hit, NEG_INF, s)
    return jnp.concatenate(vals, axis=0), jnp.concatenate(pays, axis=0)


def _peer_route_kernel(q_ref, sk_ref, exp_ref, gate_ref):
    tm = q_ref.shape[0]
    K = PEER_TOPK
    key_ids = lax.broadcasted_iota(I32, (PEER_N_KEYS, tm), 0)
    exp_rows, gate_rows = [], []
    for h in range(PEER_HEADS):
        tops = []
        for p in range(2):
            hp = 2 * h + p
            qc = q_ref[:, hp * PEER_HALF:(hp + 1) * PEER_HALF]
            s = _dot_nt(sk_ref[hp], qc)
            tops.append(_top_rows(s, key_ids, key_ids, K))
        (v0, i0), (v1, i1) = tops
        cand, cpos, cexp = [], [], []
        for a in range(K):
            nb = K // (a + 1)
            cand.append(v0[a:a + 1, :] + v1[0:nb, :])
            cpos.append(a * K + lax.broadcasted_iota(I32, (nb, tm), 0))
            cexp.append(i0[a:a + 1, :] * PEER_N_KEYS + i1[0:nb, :])
        best, experts = _top_rows(jnp.concatenate(cand, axis=0), jnp.concatenate(cpos, axis=0),
                                  jnp.concatenate(cexp, axis=0), K)
        e = jnp.exp(best - best[0:1, :])
        gate_rows.append(e / jnp.sum(e, axis=0, keepdims=True))
        exp_rows.append(experts)
    exp_ref[0] = jnp.concatenate(exp_rows, axis=0)
    gate_ref[0] = jnp.concatenate(gate_rows, axis=0)


def _peer_route(q, subkeys, tm=128):
    T = q.shape[0]
    nb = T // tm
    return pl.pallas_call(
        _peer_route_kernel,
        grid=(nb,),
        in_specs=[pl.BlockSpec((tm, q.shape[1]), lambda i: (i, 0)),
                  pl.BlockSpec(subkeys.shape, lambda i: (0, 0, 0))],
        out_specs=[pl.BlockSpec((1, PEER_E, tm), lambda i: (i, 0, 0)),
                   pl.BlockSpec((1, PEER_E, tm), lambda i: (i, 0, 0))],
        out_shape=[jax.ShapeDtypeStruct((nb, PEER_E, tm), I32),
                   jax.ShapeDtypeStruct((nb, PEER_E, tm), F32)],
        compiler_params=_cparams("parallel"),
    )(q, subkeys)


SC_LANES = 16
SC_EXPERT_BLOCK = 16
SC_RING = 4


def _sc_params():
    return dataclasses.replace(pltpu.CompilerParams(), needs_layout_passes=False)


def _sc_expert_walk(ta_hbm, tb_hbm, idx_v, bufs, sems, nt, per_block, per_token_start=None, per_token_end=None):
    EB = SC_EXPERT_BLOCK
    neb = PEER_E // EB
    nitems = nt * neb
    ring = SC_RING
    assert neb % ring == 0

    def copies(item, slot):
        iv = idx_v[item // neb, pl.ds((item % neb) * EB, EB)]
        return (pltpu.make_async_copy(ta_hbm.at[iv], bufs[2 * slot], sems[2 * slot]),
                pltpu.make_async_copy(tb_hbm.at[iv], bufs[2 * slot + 1], sems[2 * slot + 1]))

    for s in range(ring - 1):
        for cp in copies(s, s):
            cp.start()

    @pl.loop(0, nitems, step=ring)
    def _(it0):
        for s in range(ring):
            item = it0 + s
            nxt = item + ring - 1

            @pl.when(nxt < nitems)
            def _():
                for cp in copies(nxt, (s + ring - 1) % ring):
                    cp.start()

            for cp in copies(item, s):
                cp.wait()
            t = item // neb
            if s == 0 and per_token_start is not None:
                @pl.when(it0 % neb == 0)
                def _():
                    per_token_start(t)
            per_block(t, item % neb, bufs[2 * s], bufs[2 * s + 1])
            if s == ring - 1 and per_token_end is not None:
                @pl.when((it0 + ring) % neb == 0)
                def _():
                    per_token_end(t)


def _sc_ring_scratch(row_words):
    return ([pltpu.VMEM((SC_EXPERT_BLOCK, row_words), I32)] * (2 * SC_RING)
            + [pltpu.SemaphoreType.DMA] * (2 * SC_RING + 2))


def _peer_udot(ta, tb, idx, xw):
    T, E = idx.shape
    D = 2 * xw.shape[1]
    Q = D // 4
    EB = SC_EXPERT_BLOCK
    L = SC_LANES
    mesh = plsc.VectorSubcoreMesh(core_axis_name="core", subcore_axis_name="subcore")
    n_cores = mesh.num_cores
    nw = n_cores * mesh.num_subcores
    assert T % nw == 0 and E == PEER_E and ta.shape[1] == Q
    nt = T // nw

    @functools.partial(
        pl.kernel, out_type=jax.ShapeDtypeStruct((T, E), F32), mesh=mesh, compiler_params=_sc_params(),
        scratch_types=[pltpu.VMEM((nt, E), I32), pltpu.VMEM((2, D // 2), I32), pltpu.VMEM((nt, E), F32)]
                      + _sc_ring_scratch(Q))
    def udot_kernel(ta_hbm, tb_hbm, idx_hbm, x_hbm, h_hbm, idx_v, x_v, h_v, *rest):
        bufs, sems, xsems = rest[:2 * SC_RING], rest[2 * SC_RING:4 * SC_RING], rest[4 * SC_RING:]
        t0 = (lax.axis_index("subcore") * n_cores + lax.axis_index("core")) * nt
        pltpu.sync_copy(idx_hbm.at[pl.ds(t0, nt)], idx_v)
        lane = lax.iota(I32, L)

        def xcopy(t, par):
            return pltpu.make_async_copy(x_hbm.at[t0 + t], x_v.at[par], xsems[par])

        xcopy(0, 0).start()

        def token_start(t):
            for par in range(2):
                @pl.when(t % 2 == par)
                def _():
                    xcopy(t, par).wait()

                    @pl.when(t + 1 < nt)
                    def _():
                        xcopy(t + 1, 1 - par).start()

        def block(t, eb, rows_a, rows_b):
            xs = t % 2
            accs = tuple(jnp.zeros((L,), F32) for _ in range(EB))
            for half, rows in enumerate((rows_a, rows_b)):
                def cbody(c2, accs, rows=rows, half=half):
                    c = c2 * 2 * L
                    x0 = plsc.bitcast(x_v[xs, pl.ds(half * Q + c, L)], BF16)
                    x1 = plsc.bitcast(x_v[xs, pl.ds(half * Q + c + L, L)], BF16)
                    out = []
                    for e in range(EB):
                        w0 = plsc.bitcast(rows[e, pl.ds(c, L)], BF16)
                        w1 = plsc.bitcast(rows[e, pl.ds(c + L, L)], BF16)
                        pa, pb = plsc.unpack(w0 * x0 + w1 * x1, format=plsc.PackFormat.INTERLEAVED)
                        out.append(accs[e] + pa + pb)
                    return tuple(out)
                accs = lax.fori_loop(0, Q // (2 * L), cbody, accs)
            res = jnp.zeros((L,), F32)
            for e in range(EB):
                res = jnp.where(lane == e, jnp.sum(accs[e]), res)
            h_v[t, pl.ds(eb * EB, EB)] = res

        _sc_expert_walk(ta_hbm, tb_hbm, idx_v, bufs, sems, nt, block, per_token_start=token_start)
        pltpu.sync_copy(h_v, h_hbm.at[pl.ds(t0, nt)])

    return udot_kernel(ta, tb, idx, xw)


def _peer_vsum(ta, tb, idx, act):
    T, E = idx.shape
    Q = ta.shape[1]
    D = 4 * Q
    EB = SC_EXPERT_BLOCK
    L = SC_LANES
    mesh = plsc.VectorSubcoreMesh(core_axis_name="core", subcore_axis_name="subcore")
    n_cores = mesh.num_cores
    nw = n_cores * mesh.num_subcores
    assert T % nw == 0 and E == PEER_E
    nt = T // nw
    assert nt >= 2

    @functools.partial(
        pl.kernel, out_type=jax.ShapeDtypeStruct((T, D), F32), mesh=mesh, compiler_params=_sc_params(),
        scratch_types=[pltpu.VMEM((nt, E), I32), pltpu.VMEM((nt, E), F32), pltpu.VMEM((2, D), F32)]
                      + _sc_ring_scratch(Q))
    def vsum_kernel(ta_hbm, tb_hbm, idx_hbm, act_hbm, y_hbm, idx_v, act_v, y_v, *rest):
        bufs, sems, ysems = rest[:2 * SC_RING], rest[2 * SC_RING:4 * SC_RING], rest[4 * SC_RING:]
        t0 = (lax.axis_index("subcore") * n_cores + lax.axis_index("core")) * nt
        pltpu.sync_copy(idx_hbm.at[pl.ds(t0, nt)], idx_v)
        pltpu.sync_copy(act_hbm.at[pl.ds(t0, nt)], act_v)
        lane = lax.iota(I32, L)
        zero = jnp.zeros((L,), F32)

        def ycopy(t, par):
            return pltpu.make_async_copy(y_v.at[par], y_hbm.at[t0 + t], ysems[par])

        def token_start(t):
            for par in range(2):
                @pl.when((t % 2 == par) & (t >= 2))
                def _():
                    ycopy(t - 2, par).wait()

            @pl.loop(0, D // L)
            def _(c):
                y_v[t % 2, pl.ds(c * L, L)] = zero

        def token_end(t):
            for par in range(2):
                @pl.when(t % 2 == par)
                def _():
                    ycopy(t, par).start()

        def block(t, eb, rows_a, rows_b):
            ys = t % 2
            av = act_v[t, pl.ds(eb * EB, EB)]
            ab = []
            for e in range(EB):
                a = jnp.full((L,), jnp.sum(jnp.where(lane == e, av, zero)), F32)
                ab.append(plsc.pack(a, a, format=plsc.PackFormat.INTERLEAVED))
            for half, rows in enumerate((rows_a, rows_b)):
                @pl.loop(0, Q // L)
                def _(c, rows=rows, half=half):
                    alo = zero
                    ahi = zero
                    for e in range(0, EB, 2):
                        w0 = plsc.bitcast(rows[e, pl.ds(c * L, L)], BF16)
                        w1 = plsc.bitcast(rows[e + 1, pl.ds(c * L, L)], BF16)
                        pa, pb = plsc.unpack(w0 * ab[e] + w1 * ab[e + 1], format=plsc.PackFormat.INTERLEAVED)
                        alo = alo + pa
                        ahi = ahi + pb
                    o = half * 2 * Q + c * L
                    y_v[ys, pl.ds(o, L)] = y_v[ys, pl.ds(o, L)] + alo
                    y_v[ys, pl.ds(o + Q, L)] = y_v[ys, pl.ds(o + Q, L)] + ahi

        _sc_expert_walk(ta_hbm, tb_hbm, idx_v, bufs, sems, nt, block,
                        per_token_start=token_start, per_token_end=token_end)
        for tl in (nt - 2, nt - 1):
            ycopy(tl, tl % 2).wait()

    return vsum_kernel(ta, tb, idx, act)


def _peer_act_kernel(h_ref, g_ref, a_ref):
    h = h_ref[...]
    a_ref[...] = 0.5 * h * (1.0 + lax.erf(h * (2.0 ** -0.5))) * g_ref[...]


def _peer_act(h, gates, tm=512):
    T, E = h.shape
    tm = min(tm, T)
    spec = pl.BlockSpec((tm, E), lambda i: (i, 0))
    return pl.pallas_call(_peer_act_kernel, grid=(T // tm,), in_specs=[spec, spec], out_specs=spec,
                          out_shape=jax.ShapeDtypeStruct((T, E), F32),
                          compiler_params=_cparams("parallel"))(h, gates)


def _residual_norm_kernel(x_ref, y_ref, g_ref, o_ref):
    o_ref[...] = _rms(x_ref[...] + y_ref[...], g_ref[...])


def _residual_norm(x, y, g, tm=512):
    T, D = x.shape
    tm = min(tm, T)
    spec = pl.BlockSpec((tm, D), lambda i: (i, 0))
    return pl.pallas_call(_residual_norm_kernel, grid=(T // tm,),
                          in_specs=[spec, spec, pl.BlockSpec((1, D), lambda i: (0, 0))], out_specs=spec,
                          out_shape=jax.ShapeDtypeStruct((T, D), F32),
                          compiler_params=_cparams("parallel"))(x, y, g.reshape(1, D))


def _pack_bf16_pairs(w):
    Q = w.shape[1] // 4
    bits = lax.bitcast_convert_type(w.astype(BF16), jnp.uint16).astype(jnp.uint32)
    tables = []
    for s in range(2):
        words = bits[:, 2 * s * Q:(2 * s + 1) * Q] | (bits[:, (2 * s + 1) * Q:(2 * s + 2) * Q] << 16)
        tables.append(lax.bitcast_convert_type(words, I32))
    return tables


PEER_TOKEN_CHUNK = 4096
BATCH_GROUPS = 8
SC_LAG = 2


def kernel(x, mem, mix_norm_g, w_in, conv_w, conv_b, igate_b, fgate_b, mlstm_norm_g, sb_norm_g, w_out, xattn_norm_g, mem_norm_g, xattn_wq, xattn_wkv, xattn_wo, ffn_norm_g, peer_wq, peer_subkeys, peer_u, peer_v, final_norm_g):
    B, S, D = x.shape
    M = mem.shape[1]
    assert w_in.shape[0] == 1, "the output norm is applied right after the PEER sum; only one layer is supported"
    MW = MLSTM_WIDTH
    E = PEER_E
    n_groups = BATCH_GROUPS if B % BATCH_GROUPS == 0 else 1
    Bg = B // n_groups
    Tg = Bg * S

    w = w_in[0]
    gate_cols = jnp.pad(w[:, 4 * MW:4 * MW + 2 * N_MLSTM_HEADS], ((0, 0), (0, LANES - 2 * N_MLSTM_HEADS)))
    w_all = jnp.concatenate([w[:, :4 * MW], gate_cols, w[:, 4 * MW + 2 * N_MLSTM_HEADS:]], axis=1).astype(BF16)
    gate_b = jnp.pad(jnp.concatenate([igate_b[0], fgate_b[0]]), (0, LANES - 2 * N_MLSTM_HEADS))
    wo = w_out[0].astype(BF16)
    wkv = xattn_wkv[0].astype(BF16)
    wq = xattn_wq[0].astype(BF16)
    wxo = xattn_wo[0].astype(BF16)
    wpq = peer_wq[0].astype(BF16)
    subkeys = peer_subkeys[0].reshape(PEER_HEADS * 2, PEER_N_KEYS, PEER_HALF).astype(BF16)
    ua, ub = _pack_bf16_pairs(peer_u[0])
    va, vb = _pack_bf16_pairs(peer_v[0])
    Tc = min(PEER_TOKEN_CHUNK, Tg)

    outs = []
    n_chunks = Tg // Tc
    for g in range(n_groups):
        xt = x[g * Bg:(g + 1) * Bg].reshape(Tg, D)
        after = outs[(g - SC_LAG) * n_chunks:(g - SC_LAG + 1) * n_chunks] if g >= SC_LAG else ()
        qk_pre, v_m, o_m, gates, qkv_s = _norm_proj(
            xt, mix_norm_g[0], w_all, (2 * MW, MW, MW, LANES, 3 * SB_WIDTH), after=after)
        h_m = _mlstm(qk_pre, v_m, o_m, gates, conv_w[0], conv_b[0], gate_b, mlstm_norm_g[0], Bg, S)
        h_s = _sb_attention(qkv_s, sb_norm_g[0], Bg, S)
        (kv,) = _norm_proj(mem[g * Bg:(g + 1) * Bg].reshape(Bg * M, D), mem_norm_g[0], wkv, (2 * D,))
        x2 = _mixer_out_xattn(xt, h_m, h_s, wo[:MW], wo[MW:], xattn_norm_g[0], wq, kv, wxo, Bg, S, M)
        xn3, q_peer = _norm_proj(x2, ffn_norm_g[0], wpq, (PEER_HEADS * 2 * PEER_HALF,), emit_norm=True)
        experts_t, gates_t = _peer_route(q_peer, subkeys)
        experts = experts_t.transpose(0, 2, 1).reshape(Tg, E)
        gate_rows = gates_t.transpose(0, 2, 1).reshape(Tg, E)
        xw = jnp.concatenate(_pack_bf16_pairs(xn3), axis=1)
        for c in range(n_chunks):
            rows = slice(c * Tc, (c + 1) * Tc)
            idx_c = experts[rows]
            h = _peer_udot(ua, ub, idx_c, xw[rows])
            act = _peer_act(h, gate_rows[rows])
            y = _peer_vsum(va, vb, idx_c, act)
            outs.append(_residual_norm(x2[rows], y, final_norm_g))
    out = jnp.concatenate(outs, axis=0) if len(outs) > 1 else outs[0]
    return out.reshape(B, S, D)
```

```python
import dataclasses
import functools

import jax
import jax.numpy as jnp
from jax import lax
from jax.experimental import pallas as pl
from jax.experimental.pallas import tpu as pltpu
from jax.experimental.pallas import tpu_sc as plsc

F32 = jnp.float32
BF16 = jnp.bfloat16
I32 = jnp.int32

EPS = 1e-6
LANES = 128
N_MLSTM_HEADS = 4
MLSTM_HEAD_DIM = 128
MLSTM_WIDTH = N_MLSTM_HEADS * MLSTM_HEAD_DIM
N_SB_HEADS = 8
SB_HEAD_DIM = 64
SB_WIDTH = N_SB_HEADS * SB_HEAD_DIM
CONV_WIDTH = 4
MLSTM_CHUNK = 128
N_XATTN_HEADS = 4
PEER_HEADS = 8
PEER_N_KEYS = 128
PEER_TOPK = 16
PEER_HALF = 128
PEER_E = PEER_HEADS * PEER_TOPK
VMEM_LIMIT = 48 * 1024 * 1024
NEG_INF = float("-inf")


def _cparams(*sem):
    return pltpu.CompilerParams(dimension_semantics=sem, vmem_limit_bytes=VMEM_LIMIT)


def _rms(x, g):
    return x * lax.rsqrt(jnp.mean(x * x, axis=-1, keepdims=True) + EPS) * g


def _sigmoid(x):
    return 1.0 / (1.0 + jnp.exp(-x))


def _log_sigmoid(x):
    return jnp.minimum(x, 0.0) - jnp.log(1.0 + jnp.exp(-jnp.abs(x)))


def _dot(a, b):
    return jnp.dot(a.astype(BF16), b.astype(BF16), preferred_element_type=F32)


def _dot_nt(a, b):
    return lax.dot_general(a.astype(BF16), b.astype(BF16), (((1,), (1,)), ((), ())),
                           preferred_element_type=F32)


def _dot_tn(a, b):
    return lax.dot_general(a.astype(BF16), b.astype(BF16), (((0,), (0,)), ((), ())),
                           preferred_element_type=F32)


def _norm_proj_kernel(x_ref, g_ref, w_ref, *out_refs, splits, emit_norm, n_after):
    out_refs = out_refs[n_after:]
    xn = _rms(x_ref[...], g_ref[...])
    y = jnp.dot(xn.astype(BF16), w_ref[...], preferred_element_type=F32)
    outs = out_refs
    if emit_norm:
        outs[0][...] = xn
        outs = outs[1:]
    off = 0
    for o_ref, width in zip(outs, splits):
        o_ref[...] = y[:, off:off + width]
        off += width


def _norm_proj(x, g, w, splits, emit_norm=False, tm=256, after=()):
    T, D = x.shape
    N = w.shape[1]
    assert sum(splits) == N and T % tm == 0
    out_shape = [jax.ShapeDtypeStruct((T, wd), F32) for wd in splits]
    out_specs = [pl.BlockSpec((tm, wd), lambda i: (i, 0)) for wd in splits]
    if emit_norm:
        out_shape = [jax.ShapeDtypeStruct((T, D), F32)] + out_shape
        out_specs = [pl.BlockSpec((tm, D), lambda i: (i, 0))] + out_specs
    return pl.pallas_call(
        functools.partial(_norm_proj_kernel, splits=tuple(splits), emit_norm=emit_norm, n_after=len(after)),
        grid=(T // tm,),
        in_specs=[pl.BlockSpec((tm, D), lambda i: (i, 0)),
                  pl.BlockSpec((1, D), lambda i: (0, 0)),
                  pl.BlockSpec((D, N), lambda i: (0, 0))]
                 + [pl.BlockSpec((8, LANES), lambda i: (0, 0)) for _ in after],
        out_specs=out_specs,
        out_shape=out_shape,
        compiler_params=_cparams("parallel"),
    )(x, g.reshape(1, D), w, *after)


def _mlstm_kernel(qk_ref, v_ref, o_ref, gt_ref, cw_ref, cb_ref, gb_ref, ng_ref, out_ref,
                  ext_ref, ct_ref, n_ref, m_ref):
    L = MLSTM_CHUNK
    dh = MLSTM_HEAD_DIM
    c = pl.program_id(1)

    @pl.when(c == 0)
    def _():
        ext_ref[0:8, :] = jnp.zeros((8, ext_ref.shape[1]), F32)
        ct_ref[...] = jnp.zeros(ct_ref.shape, F32)
        n_ref[...] = jnp.zeros(n_ref.shape, F32)
        m_ref[...] = jnp.zeros(m_ref.shape, F32)

    a = qk_ref[...]
    ext_ref[8:8 + L, :] = a
    conv = cb_ref[...]
    for j in range(CONV_WIDTH):
        conv = conv + cw_ref[j:j + 1, :] * ext_ref[pl.ds(8 - (CONV_WIDTH - 1) + j, L), :]
    ext_ref[0:8, :] = a[L - 8:L, :]
    qk = conv * _sigmoid(conv)

    G = gt_ref[...] + gb_ref[...]
    lane = lax.broadcasted_iota(I32, G.shape, 1)
    LG = jnp.where(lane < N_MLSTM_HEADS, G, _log_sigmoid(G))
    rr = lax.broadcasted_iota(I32, (L, L), 0)
    cc = lax.broadcasted_iota(I32, (L, L), 1)
    causal = cc <= rr
    tril = causal.astype(F32)
    bcum = jnp.dot(tril, LG, preferred_element_type=F32, precision=lax.Precision.HIGHEST)
    LGT = LG.T
    bcumT = bcum.T

    for h in range(N_MLSTM_HEADS):
        q = qk[:, h * dh:(h + 1) * dh]
        k = qk[:, MLSTM_WIDTH + h * dh:MLSTM_WIDTH + (h + 1) * dh] * (dh ** -0.5)
        v = v_ref[:, h * dh:(h + 1) * dh]
        i_col = LG[:, h:h + 1]
        i_row = LGT[h:h + 1, :]
        b_col = bcum[:, N_MLSTM_HEADS + h:N_MLSTM_HEADS + h + 1]
        b_row = bcumT[N_MLSTM_HEADS + h:N_MLSTM_HEADS + h + 1, :]
        m_prev = m_ref[h:h + 1, 0:1]
        ct = ct_ref[h]
        n_row = n_ref[h:h + 1, :]

        g_col = b_col + m_prev
        Dm = jnp.where(causal, b_col - b_row + i_row, NEG_INF)
        m_t = jnp.maximum(g_col, jnp.max(Dm, axis=1, keepdims=True))
        W = jnp.exp(Dm - m_t)
        inter = jnp.exp(g_col - m_t)
        s = _dot_nt(q, k) * W
        num = inter * _dot(q, ct) + _dot(s, v)
        den = inter * jnp.sum(q * n_row, axis=1, keepdims=True) + jnp.sum(s, axis=1, keepdims=True)
        hh = num / jnp.maximum(jnp.abs(den), jnp.exp(-m_t))

        F = b_col[L - 1:L, :]
        a_col = F - b_col + i_col
        m_new = jnp.maximum(F + m_prev, jnp.max(a_col, axis=0, keepdims=True))
        decay = jnp.exp(F + m_prev - m_new)
        w_col = jnp.exp(a_col - m_new)
        ct_ref[h] = decay * ct + _dot_tn(k, v * w_col)
        n_ref[h:h + 1, :] = decay * n_row + jnp.sum(k * w_col, axis=0, keepdims=True)
        m_ref[h:h + 1, :] = jnp.broadcast_to(m_new, (1, m_ref.shape[1]))

        hh = _sigmoid(o_ref[:, h * dh:(h + 1) * dh]) * hh
        hh = hh * lax.rsqrt(jnp.mean(hh * hh, axis=1, keepdims=True) + EPS)
        out_ref[:, h * dh:(h + 1) * dh] = hh * ng_ref[:, h * dh:(h + 1) * dh]


def _mlstm(qk_pre, v_m, o_m, gates, conv_w, conv_b, gate_b, norm_g, B, S):
    L = MLSTM_CHUNK
    nc = S // L
    W2 = 2 * MLSTM_WIDTH
    row = lambda b, c: (b * nc + c, 0)
    const = lambda b, c: (0, 0)
    return pl.pallas_call(
        _mlstm_kernel,
        grid=(B, nc),
        in_specs=[pl.BlockSpec((L, W2), row),
                  pl.BlockSpec((L, MLSTM_WIDTH), row),
                  pl.BlockSpec((L, MLSTM_WIDTH), row),
                  pl.BlockSpec((L, LANES), row),
                  pl.BlockSpec((CONV_WIDTH, W2), const),
                  pl.BlockSpec((1, W2), const),
                  pl.BlockSpec((1, LANES), const),
                  pl.BlockSpec((1, MLSTM_WIDTH), const)],
        out_specs=pl.BlockSpec((L, MLSTM_WIDTH), row),
        out_shape=jax.ShapeDtypeStruct((B * S, MLSTM_WIDTH), F32),
        scratch_shapes=[pltpu.VMEM((8 + L, W2), F32),
                        pltpu.VMEM((N_MLSTM_HEADS, MLSTM_HEAD_DIM, MLSTM_HEAD_DIM), F32),
                        pltpu.VMEM((8, MLSTM_HEAD_DIM), F32),
                        pltpu.VMEM((8, LANES), F32)],
        compiler_params=_cparams("parallel", "arbitrary"),
    )(qk_pre, v_m, o_m, gates, conv_w, conv_b.reshape(1, W2), gate_b.reshape(1, LANES),
      norm_g.reshape(1, MLSTM_WIDTH))


SB_Q_TILE = 512
SB_K_TILE = 128


def _sb_kernel(q_ref, k_ref, v_ref, g_ref, out_ref, q_ref_bf, acc_ref, rb_ref, *, tq, tk):
    i = pl.program_id(2)
    d = SB_HEAD_DIM
    nsub = tq // tk
    lane = lax.broadcasted_iota(I32, (1, LANES), 1)
    first = lane < d
    q_ref_bf[...] = (q_ref[...] * (d ** -0.5)).astype(BF16)
    acc_ref[...] = jnp.zeros(acc_ref.shape, F32)
    rb_ref[...] = jnp.zeros(rb_ref.shape, F32)
    rr = lax.broadcasted_iota(I32, (tk, 2 * tk), 0)
    cc = lax.broadcasted_iota(I32, (tk, 2 * tk), 1)
    tri_ones = jnp.where((rr >= cc) | (cc >= tk), 1.0, 0.0).astype(BF16)

    def tile(j, masked):
        start = pl.multiple_of(j * tk, tk)
        kb = k_ref[pl.ds(start, tk), :]
        vb = v_ref[pl.ds(start, tk), :]
        k2 = jnp.concatenate([jnp.where(first, kb, 0.0), jnp.where(first, 0.0, kb)], axis=0).astype(BF16)
        v2 = jnp.concatenate([jnp.where(first, vb, 0.0), jnp.where(first, 0.0, vb)], axis=0).astype(BF16)
        if masked:
            strict = (j * tk + lax.broadcasted_iota(I32, (tq, tk), 1)
                      < i * tq + lax.broadcasted_iota(I32, (tq, tk), 0))
        z2 = lax.dot_general(q_ref_bf[...], k2, (((1,), (1,)), ((), ())), preferred_element_type=F32)
        ps = []
        for h in range(2):
            z = z2[:, h * tk:(h + 1) * tk]
            lsm = -(jnp.maximum(z, 0.0) + jnp.log(1.0 + jnp.exp(-jnp.abs(z))))
            if masked:
                lsm = jnp.where(strict, lsm, 0.0)
            hi = lsm.astype(BF16)
            lo = (lsm - hi.astype(F32)).astype(BF16)
            cs2 = (jnp.dot(hi, tri_ones, preferred_element_type=F32)
                   + jnp.dot(lo, tri_ones, preferred_element_type=F32))
            rb = rb_ref[h]
            log_a = z + cs2[:, :tk] + rb
            if masked:
                log_a = jnp.where(strict, log_a, NEG_INF)
            ps.append(jnp.exp(log_a).astype(BF16))
            rb_ref[h] = rb + cs2[:, tk:]
        acc_ref[...] += jnp.dot(jnp.concatenate(ps, axis=1), v2, preferred_element_type=F32)

    def masked_step(jj, carry):
        tile(i * nsub + nsub - 1 - jj, True)
        return carry

    def plain_step(jj, carry):
        tile(i * nsub - 1 - jj, False)
        return carry

    lax.fori_loop(0, nsub, masked_step, 0)
    lax.fori_loop(0, i * nsub, plain_step, 0)
    acc = acc_ref[...]
    sq = acc * acc
    ms0 = jnp.sum(jnp.where(first, sq, 0.0), axis=1, keepdims=True) * (1.0 / d)
    ms1 = jnp.sum(jnp.where(first, 0.0, sq), axis=1, keepdims=True) * (1.0 / d)
    ms = jnp.where(first, ms0, ms1)
    out_ref[...] = acc * lax.rsqrt(ms + EPS) * g_ref[...]


def _sb_attention(qkv_s, norm_g, B, S):
    tq = min(SB_Q_TILE, S)
    tk = SB_K_TILE
    nq = S // tq
    npair = SB_WIDTH // LANES
    return pl.pallas_call(
        functools.partial(_sb_kernel, tq=tq, tk=tk),
        grid=(B, npair, nq),
        in_specs=[pl.BlockSpec((tq, LANES), lambda b, p, i: (b * nq + i, p)),
                  pl.BlockSpec((S, LANES), lambda b, p, i: (b, npair + p)),
                  pl.BlockSpec((S, LANES), lambda b, p, i: (b, 2 * npair + p)),
                  pl.BlockSpec((1, LANES), lambda b, p, i: (0, p))],
        out_specs=pl.BlockSpec((tq, LANES), lambda b, p, i: (b * nq + i, p)),
        out_shape=jax.ShapeDtypeStruct((B * S, SB_WIDTH), F32),
        scratch_shapes=[pltpu.VMEM((tq, LANES), BF16),
                        pltpu.VMEM((tq, LANES), F32),
                        pltpu.VMEM((2, tq, LANES), F32)],
        compiler_params=_cparams("parallel", "parallel", "arbitrary"),
    )(qkv_s, qkv_s, qkv_s, norm_g.reshape(1, SB_WIDTH))


def _xattn_kernel(x_ref, hm_ref, hs_ref, wo1_ref, wo2_ref, g_ref, wq_ref, kv_ref, wo_ref, *rest):
    out_ref = rest[-1]
    D = x_ref.shape[1]
    dh = D // N_XATTN_HEADS
    x1 = (x_ref[...]
          + jnp.dot(hm_ref[...].astype(BF16), wo1_ref[...], preferred_element_type=F32)
          + jnp.dot(hs_ref[...].astype(BF16), wo2_ref[...], preferred_element_type=F32))
    xn = _rms(x1, g_ref[...])
    q = jnp.dot(xn.astype(BF16), wq_ref[...], preferred_element_type=F32)
    heads = []
    for h in range(N_XATTN_HEADS):
        qh = q[:, h * dh:(h + 1) * dh]
        kh = kv_ref[:, h * dh:(h + 1) * dh]
        vh = kv_ref[:, D + h * dh:D + (h + 1) * dh]
        s = _dot_nt(qh, kh) * (dh ** -0.5)
        s = s - jnp.max(s, axis=1, keepdims=True)
        p = jnp.exp(s)
        p = p / jnp.sum(p, axis=1, keepdims=True)
        heads.append(_dot(p, vh))
    o = jnp.concatenate(heads, axis=1)
    out_ref[...] = x1 + jnp.dot(o.astype(BF16), wo_ref[...], preferred_element_type=F32)


def _mixer_out_xattn(x, hm, hs, wo1, wo2, g, wq, kv, wo, B, S, M, tm=256, after=()):
    T, D = x.shape
    nt = S // tm
    row = lambda i: (i, 0)
    const = lambda i: (0, 0)
    return pl.pallas_call(
        _xattn_kernel,
        grid=(T // tm,),
        in_specs=[pl.BlockSpec((tm, D), row),
                  pl.BlockSpec((tm, MLSTM_WIDTH), row),
                  pl.BlockSpec((tm, SB_WIDTH), row),
                  pl.BlockSpec((MLSTM_WIDTH, D), const),
                  pl.BlockSpec((SB_WIDTH, D), const),
                  pl.BlockSpec((1, D), const),
                  pl.BlockSpec((D, D), const),
                  pl.BlockSpec((M, 2 * D), lambda i: (i // nt, 0)),
                  pl.BlockSpec((D, D), const)]
                 + [pl.BlockSpec((8, LANES), const) for _ in after],
        out_specs=pl.BlockSpec((tm, D), row),
        out_shape=jax.ShapeDtypeStruct((T, D), F32),
        compiler_params=_cparams("parallel"),
    )(x, hm, hs, wo1, wo2, g.reshape(1, D), wq, kv, wo, *after)


def _top_rows(s, ids, payload, k):
    big = jnp.int32(2 ** 30)
    vals, pays = [], []
    for _ in range(k):
        m = jnp.max(s, axis=0, keepdims=True)
        sel = jnp.min(jnp.where(s == m, ids, big), axis=0, keepdims=True)
        hit = ids == sel
        vals.append(m)
        pays.append(jnp.sum(jnp.where(hit, payload, 0), axis=0, keepdims=True))
        s = jnp.where(hit, NEG_INF, s)
    return jnp.concatenate(vals, axis=0), jnp.concatenate(pays, axis=0)


def _peer_route_kernel(q_ref, sk_ref, exp_ref, gate_ref):
    tm = q_ref.shape[0]
    K = PEER_TOPK
    key_ids = lax.broadcasted_iota(I32, (PEER_N_KEYS, tm), 0)
    exp_rows, gate_rows = [], []
    for h in range(PEER_HEADS):
        tops = []
        for p in range(2):
            hp = 2 * h + p
            qc = q_ref[:, hp * PEER_HALF:(hp + 1) * PEER_HALF]
            s = _dot_nt(sk_ref[hp], qc)
            tops.append(_top_rows(s, key_ids, key_ids, K))
        (v0, i0), (v1, i1) = tops
        cand, cpos, cexp = [], [], []
        for a in range(K):
            nb = K // (a + 1)
            cand.append(v0[a:a + 1, :] + v1[0:nb, :])
            cpos.append(a * K + lax.broadcasted_iota(I32, (nb, tm), 0))
            cexp.append(i0[a:a + 1, :] * PEER_N_KEYS + i1[0:nb, :])
        best, experts = _top_rows(jnp.concatenate(cand, axis=0), jnp.concatenate(cpos, axis=0),
                                  jnp.concatenate(cexp, axis=0), K)
        e = jnp.exp(best - best[0:1, :])
        gate_rows.append(e / jnp.sum(e, axis=0, keepdims=True))
        exp_rows.append(experts)
    exp_ref[0] = jnp.concatenate(exp_rows, axis=0)
    gate_ref[0] = jnp.concatenate(gate_rows, axis=0)


def _peer_route(q, subkeys, tm=128):
    T = q.shape[0]
    nb = T // tm
    return pl.pallas_call(
        _peer_route_kernel,
        grid=(nb,),
        in_specs=[pl.BlockSpec((tm, q.shape[1]), lambda i: (i, 0)),
                  pl.BlockSpec(subkeys.shape, lambda i: (0, 0, 0))],
        out_specs=[pl.BlockSpec((1, PEER_E, tm), lambda i: (i, 0, 0)),
                   pl.BlockSpec((1, PEER_E, tm), lambda i: (i, 0, 0))],
        out_shape=[jax.ShapeDtypeStruct((nb, PEER_E, tm), I32),
                   jax.ShapeDtypeStruct((nb, PEER_E, tm), F32)],
        compiler_params=_cparams("parallel"),
    )(q, subkeys)


SC_LANES = 16
SC_EXPERT_BLOCK = 16
SC_RING = 4


def _sc_params():
    return dataclasses.replace(pltpu.CompilerParams(), needs_layout_passes=False)


def _sc_expert_walk(ta_hbm, tb_hbm, idx_v, bufs, sems, nt, per_block, per_token_start=None, per_token_end=None):
    EB = SC_EXPERT_BLOCK
    neb = PEER_E // EB
    nitems = nt * neb
    ring = SC_RING
    assert neb % ring == 0

    def copies(item, slot):
        iv = idx_v[item // neb, pl.ds((item % neb) * EB, EB)]
        return (pltpu.make_async_copy(ta_hbm.at[iv], bufs[2 * slot], sems[2 * slot]),
                pltpu.make_async_copy(tb_hbm.at[iv], bufs[2 * slot + 1], sems[2 * slot + 1]))

    for s in range(ring - 1):
        for cp in copies(s, s):
            cp.start()

    @pl.loop(0, nitems, step=ring)
    def _(it0):
        for s in range(ring):
            item = it0 + s
            nxt = item + ring - 1

            @pl.when(nxt < nitems)
            def _():
                for cp in copies(nxt, (s + ring - 1) % ring):
                    cp.start()

            for cp in copies(item, s):
                cp.wait()
            t = item // neb
            if s == 0 and per_token_start is not None:
                @pl.when(it0 % neb == 0)
                def _():
                    per_token_start(t)
            per_block(t, item % neb, bufs[2 * s], bufs[2 * s + 1])
            if s == ring - 1 and per_token_end is not None:
                @pl.when((it0 + ring) % neb == 0)
                def _():
                    per_token_end(t)


def _sc_ring_scratch(row_words):
    return ([pltpu.VMEM((SC_EXPERT_BLOCK, row_words), I32)] * (2 * SC_RING)
            + [pltpu.SemaphoreType.DMA] * (2 * SC_RING + 2))


def _peer_udot(ta, tb, idx, xw):
    T, E = idx.shape
    D = 2 * xw.shape[1]
    Q = D // 4
    EB = SC_EXPERT_BLOCK
    L = SC_LANES
    mesh = plsc.VectorSubcoreMesh(core_axis_name="core", subcore_axis_name="subcore")
    n_cores = mesh.num_cores
    nw = n_cores * mesh.num_subcores
    assert T % nw == 0 and E == PEER_E and ta.shape[1] == Q
    nt = T // nw

    @functools.partial(
        pl.kernel, out_type=jax.ShapeDtypeStruct((T, E), F32), mesh=mesh, compiler_params=_sc_params(),
        scratch_types=[pltpu.VMEM((nt, E), I32), pltpu.VMEM((2, D // 2), I32), pltpu.VMEM((nt, E), F32)]
                      + _sc_ring_scratch(Q))
    def udot_kernel(ta_hbm, tb_hbm, idx_hbm, x_hbm, h_hbm, idx_v, x_v, h_v, *rest):
        bufs, sems, xsems = rest[:2 * SC_RING], rest[2 * SC_RING:4 * SC_RING], rest[4 * SC_RING:]
        t0 = (lax.axis_index("subcore") * n_cores + lax.axis_index("core")) * nt
        pltpu.sync_copy(idx_hbm.at[pl.ds(t0, nt)], idx_v)
        lane = lax.iota(I32, L)

        def xcopy(t, par):
            return pltpu.make_async_copy(x_hbm.at[t0 + t], x_v.at[par], xsems[par])

        xcopy(0, 0).start()

        def token_start(t):
            for par in range(2):
                @pl.when(t % 2 == par)
                def _():
                    xcopy(t, par).wait()

                    @pl.when(t + 1 < nt)
                    def _():
                        xcopy(t + 1, 1 - par).start()

        def block(t, eb, rows_a, rows_b):
            xs = t % 2
            accs = tuple(jnp.zeros((L,), F32) for _ in range(EB))
            for half, rows in enumerate((rows_a, rows_b)):
                def cbody(c2, accs, rows=rows, half=half):
                    c = c2 * 2 * L
                    x0 = plsc.bitcast(x_v[xs, pl.ds(half * Q + c, L)], BF16)
                    x1 = plsc.bitcast(x_v[xs, pl.ds(half * Q + c + L, L)], BF16)
                    out = []
                    for e in range(EB):
                        w0 = plsc.bitcast(rows[e, pl.ds(c, L)], BF16)
                        w1 = plsc.bitcast(rows[e, pl.ds(c + L, L)], BF16)
                        pa, pb = plsc.unpack(w0 * x0 + w1 * x1, format=plsc.PackFormat.INTERLEAVED)
                        out.append(accs[e] + pa + pb)
                    return tuple(out)
                accs = lax.fori_loop(0, Q // (2 * L), cbody, accs)
            res = jnp.zeros((L,), F32)
            for e in range(EB):
                res = jnp.where(lane == e, jnp.sum(accs[e]), res)
            h_v[t, pl.ds(eb * EB, EB)] = res

        _sc_expert_walk(ta_hbm, tb_hbm, idx_v, bufs, sems, nt, block, per_token_start=token_start)
        pltpu.sync_copy(h_v, h_hbm.at[pl.ds(t0, nt)])

    return udot_kernel(ta, tb, idx, xw)


def _peer_vsum(ta, tb, idx, act):
    T, E = idx.shape
    Q = ta.shape[1]
    D = 4 * Q
    EB = SC_EXPERT_BLOCK
    L = SC_LANES
    mesh = plsc.VectorSubcoreMesh(core_axis_name="core", subcore_axis_name="subcore")
    n_cores = mesh.num_cores
    nw = n_cores * mesh.num_subcores
    assert T % nw == 0 and E == PEER_E
    nt = T // nw
    assert nt >= 2

    @functools.partial(
        pl.kernel, out_type=jax.ShapeDtypeStruct((T, D), F32), mesh=mesh, compiler_params=_sc_params(),
        scratch_types=[pltpu.VMEM((nt, E), I32), pltpu.VMEM((nt, E), F32), pltpu.VMEM((2, D), F32)]
                      + _sc_ring_scratch(Q))
    def vsum_kernel(ta_hbm, tb_hbm, idx_hbm, act_hbm, y_hbm, idx_v, act_v, y_v, *rest):
        bufs, sems, ysems = rest[:2 * SC_RING], rest[2 * SC_RING:4 * SC_RING], rest[4 * SC_RING:]
        t0 = (lax.axis_index("subcore") * n_cores + lax.axis_index("core")) * nt
        pltpu.sync_copy(idx_hbm.at[pl.ds(t0, nt)], idx_v)
        pltpu.sync_copy(act_hbm.at[pl.ds(t0, nt)], act_v)
        lane = lax.iota(I32, L)
        zero = jnp.zeros((L,), F32)

        def ycopy(t, par):
            return pltpu.make_async_copy(y_v.at[par], y_hbm.at[t0 + t], ysems[par])

        def token_start(t):
            for par in range(2):
                @pl.when((t % 2 == par) & (t >= 2))
                def _():
                    ycopy(t - 2, par).wait()

            @pl.loop(0, D // L)
            def _(c):
                y_v[t % 2, pl.ds(c * L, L)] = zero

        def token_end(t):
            for par in range(2):
                @pl.when(t % 2 == par)
                def _():
                    ycopy(t, par).start()

        def block(t, eb, rows_a, rows_b):
            ys = t % 2
            av = act_v[t, pl.ds(eb * EB, EB)]
            ab = []
            for e in range(EB):
                a = jnp.full((L,), jnp.sum(jnp.where(lane == e, av, zero)), F32)
                ab.append(plsc.pack(a, a, format=plsc.PackFormat.INTERLEAVED))
            for half, rows in enumerate((rows_a, rows_b)):
                @pl.loop(0, Q // L)
                def _(c, rows=rows, half=half):
                    alo = zero
                    ahi = zero
                    for e in range(0, EB, 2):
                        w0 = plsc.bitcast(rows[e, pl.ds(c * L, L)], BF16)
                        w1 = plsc.bitcast(rows[e + 1, pl.ds(c * L, L)], BF16)
                        pa, pb = plsc.unpack(w0 * ab[e] + w1 * ab[e + 1], format=plsc.PackFormat.INTERLEAVED)
                        alo = alo + pa
                        ahi = ahi + pb
                    o = half * 2 * Q + c * L
                    y_v[ys, pl.ds(o, L)] = y_v[ys, pl.ds(o, L)] + alo
                    y_v[ys, pl.ds(o + Q, L)] = y_v[ys, pl.ds(o + Q, L)] + ahi

        _sc_expert_walk(ta_hbm, tb_hbm, idx_v, bufs, sems, nt, block,
                        per_token_start=token_start, per_token_end=token_end)
        for tl in (nt - 2, nt - 1):
            ycopy(tl, tl % 2).wait()

    return vsum_kernel(ta, tb, idx, act)


def _peer_act_kernel(h_ref, g_ref, a_ref):
    h = h_ref[...]
    a_ref[...] = 0.5 * h * (1.0 + lax.erf(h * (2.0 ** -0.5))) * g_ref[...]


def _peer_act(h, gates, tm=512):
    T, E = h.shape
    tm = min(tm, T)
    spec = pl.BlockSpec((tm, E), lambda i: (i, 0))
    return pl.pallas_call(_peer_act_kernel, grid=(T // tm,), in_specs=[spec, spec], out_specs=spec,
                          out_shape=jax.ShapeDtypeStruct((T, E), F32),
                          compiler_params=_cparams("parallel"))(h, gates)


def _residual_norm_kernel(x_ref, y_ref, g_ref, o_ref):
    o_ref[...] = _rms(x_ref[...] + y_ref[...], g_ref[...])


def _residual_norm(x, y, g, tm=512):
    T, D = x.shape
    tm = min(tm, T)
    spec = pl.BlockSpec((tm, D), lambda i: (i, 0))
    return pl.pallas_call(_residual_norm_kernel, grid=(T // tm,),
                          in_specs=[spec, spec, pl.BlockSpec((1, D), lambda i: (0, 0))], out_specs=spec,
                          out_shape=jax.ShapeDtypeStruct((T, D), F32),
                          compiler_params=_cparams("parallel"))(x, y, g.reshape(1, D))


def _pack_bf16_pairs(w):
    Q = w.shape[1] // 4
    bits = lax.bitcast_convert_type(w.astype(BF16), jnp.uint16).astype(jnp.uint32)
    tables = []
    for s in range(2):
        words = bits[:, 2 * s * Q:(2 * s + 1) * Q] | (bits[:, (2 * s + 1) * Q:(2 * s + 2) * Q] << 16)
        tables.append(lax.bitcast_convert_type(words, I32))
    return tables


PEER_TOKEN_CHUNK = 4096
BATCH_GROUPS = 8
SC_LAG = 3


def kernel(x, mem, mix_norm_g, w_in, conv_w, conv_b, igate_b, fgate_b, mlstm_norm_g, sb_norm_g, w_out, xattn_norm_g, mem_norm_g, xattn_wq, xattn_wkv, xattn_wo, ffn_norm_g, peer_wq, peer_subkeys, peer_u, peer_v, final_norm_g):
    B, S, D = x.shape
    M = mem.shape[1]
    assert w_in.shape[0] == 1, "the output norm is applied right after the PEER sum; only one layer is supported"
    MW = MLSTM_WIDTH
    E = PEER_E
    n_groups = BATCH_GROUPS if B % BATCH_GROUPS == 0 else 1
    Bg = B // n_groups
    Tg = Bg * S

    w = w_in[0]
    gate_cols = jnp.pad(w[:, 4 * MW:4 * MW + 2 * N_MLSTM_HEADS], ((0, 0), (0, LANES - 2 * N_MLSTM_HEADS)))
    w_all = jnp.concatenate([w[:, :4 * MW], gate_cols, w[:, 4 * MW + 2 * N_MLSTM_HEADS:]], axis=1).astype(BF16)
    gate_b = jnp.pad(jnp.concatenate([igate_b[0], fgate_b[0]]), (0, LANES - 2 * N_MLSTM_HEADS))
    wo = w_out[0].astype(BF16)
    wkv = xattn_wkv[0].astype(BF16)
    wq = xattn_wq[0].astype(BF16)
    wxo = xattn_wo[0].astype(BF16)
    wpq = peer_wq[0].astype(BF16)
    subkeys = peer_subkeys[0].reshape(PEER_HEADS * 2, PEER_N_KEYS, PEER_HALF).astype(BF16)
    ua, ub = _pack_bf16_pairs(peer_u[0])
    va, vb = _pack_bf16_pairs(peer_v[0])
    Tc = min(PEER_TOKEN_CHUNK, Tg)

    outs = []
    prev_acts = ()
    n_chunks = Tg // Tc
    for g in range(n_groups):
        xt = x[g * Bg:(g + 1) * Bg].reshape(Tg, D)
        after = outs[(g - SC_LAG) * n_chunks:(g - SC_LAG + 1) * n_chunks] if g >= SC_LAG else ()
        qk_pre, v_m, o_m, gates, qkv_s = _norm_proj(
            xt, mix_norm_g[0], w_all, (2 * MW, MW, MW, LANES, 3 * SB_WIDTH), after=after)
        h_m = _mlstm(qk_pre, v_m, o_m, gates, conv_w[0], conv_b[0], gate_b, mlstm_norm_g[0], Bg, S)
        h_s = _sb_attention(qkv_s, sb_norm_g[0], Bg, S)
        (kv,) = _norm_proj(mem[g * Bg:(g + 1) * Bg].reshape(Bg * M, D), mem_norm_g[0], wkv, (2 * D,))
        x2 = _mixer_out_xattn(xt, h_m, h_s, wo[:MW], wo[MW:], xattn_norm_g[0], wq, kv, wxo, Bg, S, M,
                              after=prev_acts)
        xn3, q_peer = _norm_proj(x2, ffn_norm_g[0], wpq, (PEER_HEADS * 2 * PEER_HALF,), emit_norm=True)
        experts_t, gates_t = _peer_route(q_peer, subkeys)
        experts = experts_t.transpose(0, 2, 1).reshape(Tg, E)
        gate_rows = gates_t.transpose(0, 2, 1).reshape(Tg, E)
        xw = jnp.concatenate(_pack_bf16_pairs(xn3), axis=1)
        acts = []
        for c in range(n_chunks):
            rows = slice(c * Tc, (c + 1) * Tc)
            idx_c = experts[rows]
            h = _peer_udot(ua, ub, idx_c, xw[rows])
            act = _peer_act(h, gate_rows[rows])
            acts.append(act)
            y = _peer_vsum(va, vb, idx_c, act)
            outs.append(_residual_norm(x2[rows], y, final_norm_g))
        prev_acts = tuple(acts)
    out = jnp.concatenate(outs, axis=0) if len(outs) > 1 else outs[0]
    return out.reshape(B, S, D)
```

```python
import dataclasses
import functools

import jax
import jax.numpy as jnp
from jax import lax
from jax.experimental import pallas as pl
from jax.experimental.pallas import tpu as pltpu
from jax.experimental.pallas import tpu_sc as plsc

F32 = jnp.float32
BF16 = jnp.bfloat16
I32 = jnp.int32

EPS = 1e-6
LANES = 128
N_MLSTM_HEADS = 4
MLSTM_HEAD_DIM = 128
MLSTM_WIDTH = N_MLSTM_HEADS * MLSTM_HEAD_DIM
N_SB_HEADS = 8
SB_HEAD_DIM = 64
SB_WIDTH = N_SB_HEADS * SB_HEAD_DIM
CONV_WIDTH = 4
MLSTM_CHUNK = 128
N_XATTN_HEADS = 4
PEER_HEADS = 8
PEER_N_KEYS = 128
PEER_TOPK = 16
PEER_HALF = 128
PEER_E = PEER_HEADS * PEER_TOPK
VMEM_LIMIT = 48 * 1024 * 1024
NEG_INF = float("-inf")


def _cparams(*sem):
    return pltpu.CompilerParams(dimension_semantics=sem, vmem_limit_bytes=VMEM_LIMIT)


def _rms(x, g):
    return x * lax.rsqrt(jnp.mean(x * x, axis=-1, keepdims=True) + EPS) * g


def _sigmoid(x):
    return 1.0 / (1.0 + jnp.exp(-x))


def _log_sigmoid(x):
    return jnp.minimum(x, 0.0) - jnp.log(1.0 + jnp.exp(-jnp.abs(x)))


def _dot(a, b):
    return jnp.dot(a.astype(BF16), b.astype(BF16), preferred_element_type=F32)


def _dot_nt(a, b):
    return lax.dot_general(a.astype(BF16), b.astype(BF16), (((1,), (1,)), ((), ())),
                           preferred_element_type=F32)


def _dot_tn(a, b):
    return lax.dot_general(a.astype(BF16), b.astype(BF16), (((0,), (0,)), ((), ())),
                           preferred_element_type=F32)


def _norm_proj_kernel(x_ref, g_ref, w_ref, *out_refs, splits, emit_norm, n_after):
    out_refs = out_refs[n_after:]
    xn = _rms(x_ref[...], g_ref[...])
    y = jnp.dot(xn.astype(BF16), w_ref[...], preferred_element_type=F32)
    outs = out_refs
    if emit_norm:
        outs[0][...] = xn
        outs = outs[1:]
    off = 0
    for o_ref, width in zip(outs, splits):
        o_ref[...] = y[:, off:off + width]
        off += width


def _norm_proj(x, g, w, splits, emit_norm=False, tm=256, after=()):
    T, D = x.shape
    N = w.shape[1]
    assert sum(splits) == N and T % tm == 0
    out_shape = [jax.ShapeDtypeStruct((T, wd), F32) for wd in splits]
    out_specs = [pl.BlockSpec((tm, wd), lambda i: (i, 0)) for wd in splits]
    if emit_norm:
        out_shape = [jax.ShapeDtypeStruct((T, D), F32)] + out_shape
        out_specs = [pl.BlockSpec((tm, D), lambda i: (i, 0))] + out_specs
    return pl.pallas_call(
        functools.partial(_norm_proj_kernel, splits=tuple(splits), emit_norm=emit_norm, n_after=len(after)),
        grid=(T // tm,),
        in_specs=[pl.BlockSpec((tm, D), lambda i: (i, 0)),
                  pl.BlockSpec((1, D), lambda i: (0, 0)),
                  pl.BlockSpec((D, N), lambda i: (0, 0))]
                 + [pl.BlockSpec((8, LANES), lambda i: (0, 0)) for _ in after],
        out_specs=out_specs,
        out_shape=out_shape,
        compiler_params=_cparams("parallel"),
    )(x, g.reshape(1, D), w, *after)


def _mlstm_kernel(qk_ref, v_ref, o_ref, gt_ref, cw_ref, cb_ref, gb_ref, ng_ref, out_ref,
                  ext_ref, ct_ref, n_ref, m_ref):
    L = MLSTM_CHUNK
    dh = MLSTM_HEAD_DIM
    c = pl.program_id(1)

    @pl.when(c == 0)
    def _():
        ext_ref[0:8, :] = jnp.zeros((8, ext_ref.shape[1]), F32)
        ct_ref[...] = jnp.zeros(ct_ref.shape, F32)
        n_ref[...] = jnp.zeros(n_ref.shape, F32)
        m_ref[...] = jnp.zeros(m_ref.shape, F32)

    a = qk_ref[...]
    ext_ref[8:8 + L, :] = a
    conv = cb_ref[...]
    for j in range(CONV_WIDTH):
        conv = conv + cw_ref[j:j + 1, :] * ext_ref[pl.ds(8 - (CONV_WIDTH - 1) + j, L), :]
    ext_ref[0:8, :] = a[L - 8:L, :]
    qk = conv * _sigmoid(conv)

    G = gt_ref[...] + gb_ref[...]
    lane = lax.broadcasted_iota(I32, G.shape, 1)
    LG = jnp.where(lane < N_MLSTM_HEADS, G, _log_sigmoid(G))
    rr = lax.broadcasted_iota(I32, (L, L), 0)
    cc = lax.broadcasted_iota(I32, (L, L), 1)
    causal = cc <= rr
    tril = causal.astype(F32)
    bcum = jnp.dot(tril, LG, preferred_element_type=F32, precision=lax.Precision.HIGHEST)
    LGT = LG.T
    bcumT = bcum.T

    for h in range(N_MLSTM_HEADS):
        q = qk[:, h * dh:(h + 1) * dh]
        k = qk[:, MLSTM_WIDTH + h * dh:MLSTM_WIDTH + (h + 1) * dh] * (dh ** -0.5)
        v = v_ref[:, h * dh:(h + 1) * dh]
        i_col = LG[:, h:h + 1]
        i_row = LGT[h:h + 1, :]
        b_col = bcum[:, N_MLSTM_HEADS + h:N_MLSTM_HEADS + h + 1]
        b_row = bcumT[N_MLSTM_HEADS + h:N_MLSTM_HEADS + h + 1, :]
        m_prev = m_ref[h:h + 1, 0:1]
        ct = ct_ref[h]
        n_row = n_ref[h:h + 1, :]

        g_col = b_col + m_prev
        Dm = jnp.where(causal, b_col - b_row + i_row, NEG_INF)
        m_t = jnp.maximum(g_col, jnp.max(Dm, axis=1, keepdims=True))
        W = jnp.exp(Dm - m_t)
        inter = jnp.exp(g_col - m_t)
        s = _dot_nt(q, k) * W
        num = inter * _dot(q, ct) + _dot(s, v)
        den = inter * jnp.sum(q * n_row, axis=1, keepdims=True) + jnp.sum(s, axis=1, keepdims=True)
        hh = num / jnp.maximum(jnp.abs(den), jnp.exp(-m_t))

        F = b_col[L - 1:L, :]
        a_col = F - b_col + i_col
        m_new = jnp.maximum(F + m_prev, jnp.max(a_col, axis=0, keepdims=True))
        decay = jnp.exp(F + m_prev - m_new)
        w_col = jnp.exp(a_col - m_new)
        ct_ref[h] = decay * ct + _dot_tn(k, v * w_col)
        n_ref[h:h + 1, :] = decay * n_row + jnp.sum(k * w_col, axis=0, keepdims=True)
        m_ref[h:h + 1, :] = jnp.broadcast_to(m_new, (1, m_ref.shape[1]))

        hh = _sigmoid(o_ref[:, h * dh:(h + 1) * dh]) * hh
        hh = hh * lax.rsqrt(jnp.mean(hh * hh, axis=1, keepdims=True) + EPS)
        out_ref[:, h * dh:(h + 1) * dh] = hh * ng_ref[:, h * dh:(h + 1) * dh]


def _mlstm(qk_pre, v_m, o_m, gates, conv_w, conv_b, gate_b, norm_g, B, S):
    L = MLSTM_CHUNK
    nc = S // L
    W2 = 2 * MLSTM_WIDTH
    row = lambda b, c: (b * nc + c, 0)
    const = lambda b, c: (0, 0)
    return pl.pallas_call(
        _mlstm_kernel,
        grid=(B, nc),
        in_specs=[pl.BlockSpec((L, W2), row),
                  pl.BlockSpec((L, MLSTM_WIDTH), row),
                  pl.BlockSpec((L, MLSTM_WIDTH), row),
                  pl.BlockSpec((L, LANES), row),
                  pl.BlockSpec((CONV_WIDTH, W2), const),
                  pl.BlockSpec((1, W2), const),
                  pl.BlockSpec((1, LANES), const),
                  pl.BlockSpec((1, MLSTM_WIDTH), const)],
        out_specs=pl.BlockSpec((L, MLSTM_WIDTH), row),
        out_shape=jax.ShapeDtypeStruct((B * S, MLSTM_WIDTH), F32),
        scratch_shapes=[pltpu.VMEM((8 + L, W2), F32),
                        pltpu.VMEM((N_MLSTM_HEADS, MLSTM_HEAD_DIM, MLSTM_HEAD_DIM), F32),
                        pltpu.VMEM((8, MLSTM_HEAD_DIM), F32),
                        pltpu.VMEM((8, LANES), F32)],
        compiler_params=_cparams("parallel", "arbitrary"),
    )(qk_pre, v_m, o_m, gates, conv_w, conv_b.reshape(1, W2), gate_b.reshape(1, LANES),
      norm_g.reshape(1, MLSTM_WIDTH))


SB_Q_TILE = 1024
SB_K_TILE = 128


def _sb_kernel(q_ref, k_ref, v_ref, g_ref, out_ref, q_ref_bf, acc_ref, rb_ref, *, tq, tk):
    i = pl.program_id(2)
    d = SB_HEAD_DIM
    nsub = tq // tk
    lane = lax.broadcasted_iota(I32, (1, LANES), 1)
    first = lane < d
    q_ref_bf[...] = (q_ref[...] * (d ** -0.5)).astype(BF16)
    acc_ref[...] = jnp.zeros(acc_ref.shape, F32)
    rb_ref[...] = jnp.zeros(rb_ref.shape, F32)
    rr = lax.broadcasted_iota(I32, (tk, 2 * tk), 0)
    cc = lax.broadcasted_iota(I32, (tk, 2 * tk), 1)
    tri_ones = jnp.where((rr >= cc) | (cc >= tk), 1.0, 0.0).astype(BF16)

    def tile(j, masked):
        start = pl.multiple_of(j * tk, tk)
        kb = k_ref[pl.ds(start, tk), :]
        vb = v_ref[pl.ds(start, tk), :]
        k2 = jnp.concatenate([jnp.where(first, kb, 0.0), jnp.where(first, 0.0, kb)], axis=0).astype(BF16)
        v2 = jnp.concatenate([jnp.where(first, vb, 0.0), jnp.where(first, 0.0, vb)], axis=0).astype(BF16)
        if masked:
            strict = (j * tk + lax.broadcasted_iota(I32, (tq, tk), 1)
                      < i * tq + lax.broadcasted_iota(I32, (tq, tk), 0))
        z2 = lax.dot_general(q_ref_bf[...], k2, (((1,), (1,)), ((), ())), preferred_element_type=F32)
        ps = []
        for h in range(2):
            z = z2[:, h * tk:(h + 1) * tk]
            lsm = -(jnp.maximum(z, 0.0) + jnp.log(1.0 + jnp.exp(-jnp.abs(z))))
            if masked:
                lsm = jnp.where(strict, lsm, 0.0)
            hi = lsm.astype(BF16)
            lo = (lsm - hi.astype(F32)).astype(BF16)
            cs2 = (jnp.dot(hi, tri_ones, preferred_element_type=F32)
                   + jnp.dot(lo, tri_ones, preferred_element_type=F32))
            rb = rb_ref[h]
            log_a = z + cs2[:, :tk] + rb
            if masked:
                log_a = jnp.where(strict, log_a, NEG_INF)
            ps.append(jnp.exp(log_a).astype(BF16))
            rb_ref[h] = rb + cs2[:, tk:]
        acc_ref[...] += jnp.dot(jnp.concatenate(ps, axis=1), v2, preferred_element_type=F32)

    def masked_step(jj, carry):
        tile(i * nsub + nsub - 1 - jj, True)
        return carry

    def plain_step(jj, carry):
        tile(i * nsub - 1 - jj, False)
        return carry

    lax.fori_loop(0, nsub, masked_step, 0)
    lax.fori_loop(0, i * nsub, plain_step, 0)
    acc = acc_ref[...]
    sq = acc * acc
    ms0 = jnp.sum(jnp.where(first, sq, 0.0), axis=1, keepdims=True) * (1.0 / d)
    ms1 = jnp.sum(jnp.where(first, 0.0, sq), axis=1, keepdims=True) * (1.0 / d)
    ms = jnp.where(first, ms0, ms1)
    out_ref[...] = acc * lax.rsqrt(ms + EPS) * g_ref[...]


def _sb_attention(qkv_s, norm_g, B, S):
    tq = min(SB_Q_TILE, S)
    tk = SB_K_TILE
    nq = S // tq
    npair = SB_WIDTH // LANES
    return pl.pallas_call(
        functools.partial(_sb_kernel, tq=tq, tk=tk),
        grid=(B, npair, nq),
        in_specs=[pl.BlockSpec((tq, LANES), lambda b, p, i: (b * nq + i, p)),
                  pl.BlockSpec((S, LANES), lambda b, p, i: (b, npair + p)),
                  pl.BlockSpec((S, LANES), lambda b, p, i: (b, 2 * npair + p)),
                  pl.BlockSpec((1, LANES), lambda b, p, i: (0, p))],
        out_specs=pl.BlockSpec((tq, LANES), lambda b, p, i: (b * nq + i, p)),
        out_shape=jax.ShapeDtypeStruct((B * S, SB_WIDTH), F32),
        scratch_shapes=[pltpu.VMEM((tq, LANES), BF16),
                        pltpu.VMEM((tq, LANES), F32),
                        pltpu.VMEM((2, tq, LANES), F32)],
        compiler_params=_cparams("parallel", "parallel", "arbitrary"),
    )(qkv_s, qkv_s, qkv_s, norm_g.reshape(1, SB_WIDTH))


def _xattn_kernel(x_ref, hm_ref, hs_ref, wo1_ref, wo2_ref, g_ref, wq_ref, kv_ref, wo_ref, *rest):
    out_ref = rest[-1]
    D = x_ref.shape[1]
    dh = D // N_XATTN_HEADS
    x1 = (x_ref[...]
          + jnp.dot(hm_ref[...].astype(BF16), wo1_ref[...], preferred_element_type=F32)
          + jnp.dot(hs_ref[...].astype(BF16), wo2_ref[...], preferred_element_type=F32))
    xn = _rms(x1, g_ref[...])
    q = jnp.dot(xn.astype(BF16), wq_ref[...], preferred_element_type=F32)
    heads = []
    for h in range(N_XATTN_HEADS):
        qh = q[:, h * dh:(h + 1) * dh]
        kh = kv_ref[:, h * dh:(h + 1) * dh]
        vh = kv_ref[:, D + h * dh:D + (h + 1) * dh]
        s = _dot_nt(qh, kh) * (dh ** -0.5)
        s = s - jnp.max(s, axis=1, keepdims=True)
        p = jnp.exp(s)
        p = p / jnp.sum(p, axis=1, keepdims=True)
        heads.append(_dot(p, vh))
    o = jnp.concatenate(heads, axis=1)
    out_ref[...] = x1 + jnp.dot(o.astype(BF16), wo_ref[...], preferred_element_type=F32)


def _mixer_out_xattn(x, hm, hs, wo1, wo2, g, wq, kv, wo, B, S, M, tm=256, after=()):
    T, D = x.shape
    nt = S // tm
    row = lambda i: (i, 0)
    const = lambda i: (0, 0)
    return pl.pallas_call(
        _xattn_kernel,
        grid=(T // tm,),
        in_specs=[pl.BlockSpec((tm, D), row),
                  pl.BlockSpec((tm, MLSTM_WIDTH), row),
                  pl.BlockSpec((tm, SB_WIDTH), row),
                  pl.BlockSpec((MLSTM_WIDTH, D), const),
                  pl.BlockSpec((SB_WIDTH, D), const),
                  pl.BlockSpec((1, D), const),
                  pl.BlockSpec((D, D), const),
                  pl.BlockSpec((M, 2 * D), lambda i: (i // nt, 0)),
                  pl.BlockSpec((D, D), const)]
                 + [pl.BlockSpec((8, LANES), const) for _ in after],
        out_specs=pl.BlockSpec((tm, D), row),
        out_shape=jax.ShapeDtypeStruct((T, D), F32),
        compiler_params=_cparams("parallel"),
    )(x, hm, hs, wo1, wo2, g.reshape(1, D), wq, kv, wo, *after)


def _top_rows(s, ids, payload, k):
    big = jnp.int32(2 ** 30)
    vals, pays = [], []
    for _ in range(k):
        m = jnp.max(s, axis=0, keepdims=True)
        sel = jnp.min(jnp.where(s == m, ids, big), axis=0, keepdims=True)
        hit = ids == sel
        vals.append(m)
        pays.append(jnp.sum(jnp.where(hit, payload, 0), axis=0, keepdims=True))
        s = jnp.where(hit, NEG_INF, s)
    return jnp.concatenate(vals, axis=0), jnp.concatenate(pays, axis=0)


def _peer_route_kernel(q_ref, sk_ref, exp_ref, gate_ref):
    tm = q_ref.shape[0]
    K = PEER_TOPK
    key_ids = lax.broadcasted_iota(I32, (PEER_N_KEYS, tm), 0)
    exp_rows, gate_rows = [], []
    for h in range(PEER_HEADS):
        tops = []
        for p in range(2):
            hp = 2 * h + p
            qc = q_ref[:, hp * PEER_HALF:(hp + 1) * PEER_HALF]
            s = _dot_nt(sk_ref[hp], qc)
            tops.append(_top_rows(s, key_ids, key_ids, K))
        (v0, i0), (v1, i1) = tops
        cand, cpos, cexp = [], [], []
        for a in range(K):
            nb = K // (a + 1)
            cand.append(v0[a:a + 1, :] + v1[0:nb, :])
            cpos.append(a * K + lax.broadcasted_iota(I32, (nb, tm), 0))
            cexp.append(i0[a:a + 1, :] * PEER_N_KEYS + i1[0:nb, :])
        best, experts = _top_rows(jnp.concatenate(cand, axis=0), jnp.concatenate(cpos, axis=0),
                                  jnp.concatenate(cexp, axis=0), K)
        e = jnp.exp(best - best[0:1, :])
        gate_rows.append(e / jnp.sum(e, axis=0, keepdims=True))
        exp_rows.append(experts)
    exp_ref[0] = jnp.concatenate(exp_rows, axis=0)
    gate_ref[0] = jnp.concatenate(gate_rows, axis=0)


def _peer_route(q, subkeys, tm=128):
    T = q.shape[0]
    nb = T // tm
    return pl.pallas_call(
        _peer_route_kernel,
        grid=(nb,),
        in_specs=[pl.BlockSpec((tm, q.shape[1]), lambda i: (i, 0)),
                  pl.BlockSpec(subkeys.shape, lambda i: (0, 0, 0))],
        out_specs=[pl.BlockSpec((1, PEER_E, tm), lambda i: (i, 0, 0)),
                   pl.BlockSpec((1, PEER_E, tm), lambda i: (i, 0, 0))],
        out_shape=[jax.ShapeDtypeStruct((nb, PEER_E, tm), I32),
                   jax.ShapeDtypeStruct((nb, PEER_E, tm), F32)],
        compiler_params=_cparams("parallel"),
    )(q, subkeys)


SC_LANES = 16
SC_EXPERT_BLOCK = 16
SC_RING = 4


def _sc_params():
    return dataclasses.replace(pltpu.CompilerParams(), needs_layout_passes=False)


def _sc_expert_walk(ta_hbm, tb_hbm, idx_v, bufs, sems, nt, per_block, per_token_start=None, per_token_end=None):
    EB = SC_EXPERT_BLOCK
    neb = PEER_E // EB
    nitems = nt * neb
    ring = SC_RING
    assert neb % ring == 0

    def copies(item, slot):
        iv = idx_v[item // neb, pl.ds((item % neb) * EB, EB)]
        return (pltpu.make_async_copy(ta_hbm.at[iv], bufs[2 * slot], sems[2 * slot]),
                pltpu.make_async_copy(tb_hbm.at[iv], bufs[2 * slot + 1], sems[2 * slot + 1]))

    for s in range(ring - 1):
        for cp in copies(s, s):
            cp.start()

    @pl.loop(0, nitems, step=ring)
    def _(it0):
        for s in range(ring):
            item = it0 + s
            nxt = item + ring - 1

            @pl.when(nxt < nitems)
            def _():
                for cp in copies(nxt, (s + ring - 1) % ring):
                    cp.start()

            for cp in copies(item, s):
                cp.wait()
            t = item // neb
            if s == 0 and per_token_start is not None:
                @pl.when(it0 % neb == 0)
                def _():
                    per_token_start(t)
            per_block(t, item % neb, bufs[2 * s], bufs[2 * s + 1])
            if s == ring - 1 and per_token_end is not None:
                @pl.when((it0 + ring) % neb == 0)
                def _():
                    per_token_end(t)


def _sc_ring_scratch(row_words):
    return ([pltpu.VMEM((SC_EXPERT_BLOCK, row_words), I32)] * (2 * SC_RING)
            + [pltpu.SemaphoreType.DMA] * (2 * SC_RING + 2))


def _peer_udot(ta, tb, idx, xw):
    T, E = idx.shape
    D = 2 * xw.shape[1]
    Q = D // 4
    EB = SC_EXPERT_BLOCK
    L = SC_LANES
    mesh = plsc.VectorSubcoreMesh(core_axis_name="core", subcore_axis_name="subcore")
    n_cores = mesh.num_cores
    nw = n_cores * mesh.num_subcores
    assert T % nw == 0 and E == PEER_E and ta.shape[1] == Q
    nt = T // nw

    @functools.partial(
        pl.kernel, out_type=jax.ShapeDtypeStruct((T, E), F32), mesh=mesh, compiler_params=_sc_params(),
        scratch_types=[pltpu.VMEM((nt, E), I32), pltpu.VMEM((2, D // 2), I32), pltpu.VMEM((nt, E), F32)]
                      + _sc_ring_scratch(Q))
    def udot_kernel(ta_hbm, tb_hbm, idx_hbm, x_hbm, h_hbm, idx_v, x_v, h_v, *rest):
        bufs, sems, xsems = rest[:2 * SC_RING], rest[2 * SC_RING:4 * SC_RING], rest[4 * SC_RING:]
        t0 = (lax.axis_index("subcore") * n_cores + lax.axis_index("core")) * nt
        pltpu.sync_copy(idx_hbm.at[pl.ds(t0, nt)], idx_v)
        lane = lax.iota(I32, L)

        def xcopy(t, par):
            return pltpu.make_async_copy(x_hbm.at[t0 + t], x_v.at[par], xsems[par])

        xcopy(0, 0).start()

        def token_start(t):
            for par in range(2):
                @pl.when(t % 2 == par)
                def _():
                    xcopy(t, par).wait()

                    @pl.when(t + 1 < nt)
                    def _():
                        xcopy(t + 1, 1 - par).start()

        def block(t, eb, rows_a, rows_b):
            xs = t % 2
            accs = tuple(jnp.zeros((L,), F32) for _ in range(EB))
            for half, rows in enumerate((rows_a, rows_b)):
                def cbody(c2, accs, rows=rows, half=half):
                    c = c2 * 2 * L
                    x0 = plsc.bitcast(x_v[xs, pl.ds(half * Q + c, L)], BF16)
                    x1 = plsc.bitcast(x_v[xs, pl.ds(half * Q + c + L, L)], BF16)
                    out = []
                    for e in range(EB):
                        w0 = plsc.bitcast(rows[e, pl.ds(c, L)], BF16)
                        w1 = plsc.bitcast(rows[e, pl.ds(c + L, L)], BF16)
                        pa, pb = plsc.unpack(w0 * x0 + w1 * x1, format=plsc.PackFormat.INTERLEAVED)
                        out.append(accs[e] + pa + pb)
                    return tuple(out)
                accs = lax.fori_loop(0, Q // (2 * L), cbody, accs)
            res = jnp.zeros((L,), F32)
            for e in range(EB):
                res = jnp.where(lane == e, jnp.sum(accs[e]), res)
            h_v[t, pl.ds(eb * EB, EB)] = res

        _sc_expert_walk(ta_hbm, tb_hbm, idx_v, bufs, sems, nt, block, per_token_start=token_start)
        pltpu.sync_copy(h_v, h_hbm.at[pl.ds(t0, nt)])

    return udot_kernel(ta, tb, idx, xw)


def _peer_vsum(ta, tb, idx, act):
    T, E = idx.shape
    Q = ta.shape[1]
    D = 4 * Q
    EB = SC_EXPERT_BLOCK
    L = SC_LANES
    mesh = plsc.VectorSubcoreMesh(core_axis_name="core", subcore_axis_name="subcore")
    n_cores = mesh.num_cores
    nw = n_cores * mesh.num_subcores
    assert T % nw == 0 and E == PEER_E
    nt = T // nw
    assert nt >= 2

    @functools.partial(
        pl.kernel, out_type=jax.ShapeDtypeStruct((T, D), F32), mesh=mesh, compiler_params=_sc_params(),
        scratch_types=[pltpu.VMEM((nt, E), I32), pltpu.VMEM((nt, E), F32), pltpu.VMEM((2, D), F32)]
                      + _sc_ring_scratch(Q))
    def vsum_kernel(ta_hbm, tb_hbm, idx_hbm, act_hbm, y_hbm, idx_v, act_v, y_v, *rest):
        bufs, sems, ysems = rest[:2 * SC_RING], rest[2 * SC_RING:4 * SC_RING], rest[4 * SC_RING:]
        t0 = (lax.axis_index("subcore") * n_cores + lax.axis_index("core")) * nt
        pltpu.sync_copy(idx_hbm.at[pl.ds(t0, nt)], idx_v)
        pltpu.sync_copy(act_hbm.at[pl.ds(t0, nt)], act_v)
        lane = lax.iota(I32, L)
        zero = jnp.zeros((L,), F32)

        def ycopy(t, par):
            return pltpu.make_async_copy(y_v.at[par], y_hbm.at[t0 + t], ysems[par])

        def token_start(t):
            for par in range(2):
                @pl.when((t % 2 == par) & (t >= 2))
                def _():
                    ycopy(t - 2, par).wait()

            @pl.loop(0, D // L)
            def _(c):
                y_v[t % 2, pl.ds(c * L, L)] = zero

        def token_end(t):
            for par in range(2):
                @pl.when(t % 2 == par)
                def _():
                    ycopy(t, par).start()

        def block(t, eb, rows_a, rows_b):
            ys = t % 2
            av = act_v[t, pl.ds(eb * EB, EB)]
            ab = []
            for e in range(EB):
                a = jnp.full((L,), jnp.sum(jnp.where(lane == e, av, zero)), F32)
                ab.append(plsc.pack(a, a, format=plsc.PackFormat.INTERLEAVED))
            for half, rows in enumerate((rows_a, rows_b)):
                @pl.loop(0, Q // L)
                def _(c, rows=rows, half=half):
                    alo = zero
                    ahi = zero
                    for e in range(0, EB, 2):
                        w0 = plsc.bitcast(rows[e, pl.ds(c * L, L)], BF16)
                        w1 = plsc.bitcast(rows[e + 1, pl.ds(c * L, L)], BF16)
                        pa, pb = plsc.unpack(w0 * ab[e] + w1 * ab[e + 1], format=plsc.PackFormat.INTERLEAVED)
                        alo = alo + pa
                        ahi = ahi + pb
                    o = half * 2 * Q + c * L
                    y_v[ys, pl.ds(o, L)] = y_v[ys, pl.ds(o, L)] + alo
                    y_v[ys, pl.ds(o + Q, L)] = y_v[ys, pl.ds(o + Q, L)] + ahi

        _sc_expert_walk(ta_hbm, tb_hbm, idx_v, bufs, sems, nt, block,
                        per_token_start=token_start, per_token_end=token_end)
        for tl in (nt - 2, nt - 1):
            ycopy(tl, tl % 2).wait()

    return vsum_kernel(ta, tb, idx, act)


def _peer_act_kernel(h_ref, g_ref, a_ref):
    h = h_ref[...]
    a_ref[...] = 0.5 * h * (1.0 + lax.erf(h * (2.0 ** -0.5))) * g_ref[...]


def _peer_act(h, gates, tm=512):
    T, E = h.shape
    tm = min(tm, T)
    spec = pl.BlockSpec((tm, E), lambda i: (i, 0))
    return pl.pallas_call(_peer_act_kernel, grid=(T // tm,), in_specs=[spec, spec], out_specs=spec,
                          out_shape=jax.ShapeDtypeStruct((T, E), F32),
                          compiler_params=_cparams("parallel"))(h, gates)


def _residual_norm_kernel(x_ref, y_ref, g_ref, o_ref):
    o_ref[...] = _rms(x_ref[...] + y_ref[...], g_ref[...])


def _residual_norm(x, y, g, tm=512):
    T, D = x.shape
    tm = min(tm, T)
    spec = pl.BlockSpec((tm, D), lambda i: (i, 0))
    return pl.pallas_call(_residual_norm_kernel, grid=(T // tm,),
                          in_specs=[spec, spec, pl.BlockSpec((1, D), lambda i: (0, 0))], out_specs=spec,
                          out_shape=jax.ShapeDtypeStruct((T, D), F32),
                          compiler_params=_cparams("parallel"))(x, y, g.reshape(1, D))


def _pack_bf16_pairs(w):
    Q = w.shape[1] // 4
    bits = lax.bitcast_convert_type(w.astype(BF16), jnp.uint16).astype(jnp.uint32)
    tables = []
    for s in range(2):
        words = bits[:, 2 * s * Q:(2 * s + 1) * Q] | (bits[:, (2 * s + 1) * Q:(2 * s + 2) * Q] << 16)
        tables.append(lax.bitcast_convert_type(words, I32))
    return tables


PEER_TOKEN_CHUNK = 4096
BATCH_GROUPS = 8
SC_LAG = 3


def kernel(x, mem, mix_norm_g, w_in, conv_w, conv_b, igate_b, fgate_b, mlstm_norm_g, sb_norm_g, w_out, xattn_norm_g, mem_norm_g, xattn_wq, xattn_wkv, xattn_wo, ffn_norm_g, peer_wq, peer_subkeys, peer_u, peer_v, final_norm_g):
    B, S, D = x.shape
    M = mem.shape[1]
    assert w_in.shape[0] == 1, "the output norm is applied right after the PEER sum; only one layer is supported"
    MW = MLSTM_WIDTH
    E = PEER_E
    n_groups = BATCH_GROUPS if B % BATCH_GROUPS == 0 else 1
    Bg = B // n_groups
    Tg = Bg * S

    w = w_in[0]
    gate_cols = jnp.pad(w[:, 4 * MW:4 * MW + 2 * N_MLSTM_HEADS], ((0, 0), (0, LANES - 2 * N_MLSTM_HEADS)))
    w_all = jnp.concatenate([w[:, :4 * MW], gate_cols, w[:, 4 * MW + 2 * N_MLSTM_HEADS:]], axis=1).astype(BF16)
    gate_b = jnp.pad(jnp.concatenate([igate_b[0], fgate_b[0]]), (0, LANES - 2 * N_MLSTM_HEADS))
    wo = w_out[0].astype(BF16)
    wkv = xattn_wkv[0].astype(BF16)
    wq = xattn_wq[0].astype(BF16)
    wxo = xattn_wo[0].astype(BF16)
    wpq = peer_wq[0].astype(BF16)
    subkeys = peer_subkeys[0].reshape(PEER_HEADS * 2, PEER_N_KEYS, PEER_HALF).astype(BF16)
    ua, ub = _pack_bf16_pairs(peer_u[0])
    va, vb = _pack_bf16_pairs(peer_v[0])
    Tc = min(PEER_TOKEN_CHUNK, Tg)

    outs = []
    prev_acts = ()
    n_chunks = Tg // Tc
    for g in range(n_groups):
        xt = x[g * Bg:(g + 1) * Bg].reshape(Tg, D)
        after = outs[(g - SC_LAG) * n_chunks:(g - SC_LAG + 1) * n_chunks] if g >= SC_LAG else ()
        qk_pre, v_m, o_m, gates, qkv_s = _norm_proj(
            xt, mix_norm_g[0], w_all, (2 * MW, MW, MW, LANES, 3 * SB_WIDTH), after=after)
        h_m = _mlstm(qk_pre, v_m, o_m, gates, conv_w[0], conv_b[0], gate_b, mlstm_norm_g[0], Bg, S)
        h_s = _sb_attention(qkv_s, sb_norm_g[0], Bg, S)
        (kv,) = _norm_proj(mem[g * Bg:(g + 1) * Bg].reshape(Bg * M, D), mem_norm_g[0], wkv, (2 * D,))
        x2 = _mixer_out_xattn(xt, h_m, h_s, wo[:MW], wo[MW:], xattn_norm_g[0], wq, kv, wxo, Bg, S, M,
                              after=prev_acts)
        xn3, q_peer = _norm_proj(x2, ffn_norm_g[0], wpq, (PEER_HEADS * 2 * PEER_HALF,), emit_norm=True)
        experts_t, gates_t = _peer_route(q_peer, subkeys)
        experts = experts_t.transpose(0, 2, 1).reshape(Tg, E)
        gate_rows = gates_t.transpose(0, 2, 1).reshape(Tg, E)
        xw = jnp.concatenate(_pack_bf16_pairs(xn3), axis=1)
        acts = []
        for c in range(n_chunks):
            rows = slice(c * Tc, (c + 1) * Tc)
            idx_c = experts[rows]
            h = _peer_udot(ua, ub, idx_c, xw[rows])
            act = _peer_act(h, gate_rows[rows])
            acts.append(act)
            y = _peer_vsum(va, vb, idx_c, act)
            outs.append(_residual_norm(x2[rows], y, final_norm_g))
        prev_acts = tuple(acts)
    out = jnp.concatenate(outs, axis=0) if len(outs) > 1 else outs[0]
    return out.reshape(B, S, D)
```
